```python
import math
import jax, jax.numpy as jnp
from jax import lax
import numpy as np

D_MODEL = 1024
BATCH = 8
SEQ = 4096
DEPTH = 1
DEC_BATCH = 32
DEC_SEQ = 2048
PAST_LEN = 128

N_HEADS = 8
HEAD_DIM = 64
V_DIM = 2 * HEAD_DIM
QK_WIDTH = 2 * N_HEADS * HEAD_DIM
ATTN_WIDTH = N_HEADS * V_DIM
ROPE_THETA = 10000.0
Q_BLOCK = 128
RMS_EPS = 1e-5
LRU_WIDTH = 1024
LRU_BLOCKS = 16
LRU_BLOCK = LRU_WIDTH // LRU_BLOCKS
LRU_C = 8.0
CONV_WIDTH = 4
CONV_LEFT = 2
GATE_WIDTH = D_MODEL
IN_WIDTH = 2 * QK_WIDTH + ATTN_WIDTH + 2 * LRU_WIDTH + 2 * GATE_WIDTH
IN_SPLITS = [QK_WIDTH, 2 * QK_WIDTH, 2 * QK_WIDTH + ATTN_WIDTH,
             2 * QK_WIDTH + ATTN_WIDTH + LRU_WIDTH,
             2 * QK_WIDTH + ATTN_WIDTH + 2 * LRU_WIDTH,
             2 * QK_WIDTH + ATTN_WIDTH + 2 * LRU_WIDTH + GATE_WIDTH]
D_FF = ((8 * D_MODEL + 3 * 256 - 1) // (3 * 256)) * 256
ALPHA = (2.0 * DEPTH) ** 0.25
BETA = (8.0 * DEPTH) ** -0.25
LN_EPS = 1e-5

kernel_name = 'hybrid_diffattn_rglru_deepnorm_encoder'


def layer_norm(x, g=None, b=None):
    xf = x.astype(jnp.float32)
    mu = jnp.mean(xf, axis=-1, keepdims=True)
    var = jnp.mean(jnp.square(xf - mu), axis=-1, keepdims=True)
    y = (xf - mu) * lax.rsqrt(var + LN_EPS)
    if g is not None:
        y = y * g.astype(jnp.float32) + b.astype(jnp.float32)
    return y.astype(x.dtype)


def rope_tables(seq):
    inv = 1.0 / (ROPE_THETA ** (jnp.arange(0, HEAD_DIM, 2, dtype=jnp.float32) / HEAD_DIM))
    ang = jnp.arange(seq, dtype=jnp.float32)[:, None] * inv[None, :]
    return jnp.cos(ang), jnp.sin(ang)


def apply_rope(t, cos, sin):
    tf = t.astype(jnp.float32)
    t1, t2 = jnp.split(tf, 2, axis=-1)
    c = cos[:, None, :]
    s = sin[:, None, :]
    return jnp.concatenate([t1 * c - t2 * s, t2 * c + t1 * s], axis=-1).astype(t.dtype)


def diff_attention(q, k, v, lam, lambda_init, subln_g):
    B, S = q.shape[0], q.shape[1]
    q = q.reshape(B, S, N_HEADS, 2, HEAD_DIM) * (HEAD_DIM ** -0.5)
    k = k.reshape(B, S, N_HEADS, 2, HEAD_DIM)
    nblk = S // Q_BLOCK
    qb = q.reshape(B, nblk, Q_BLOCK, N_HEADS, 2, HEAD_DIM).transpose(1, 0, 2, 3, 4, 5)

    def block(qi):
        s = jnp.einsum('bqhcd,bkhcd->bhcqk', qi, k).astype(jnp.float32)
        p = jax.nn.softmax(s, axis=-1)
        w = p[:, :, 0] - lam * p[:, :, 1]
        return jnp.einsum('bhqk,bkhe->bqhe', w.astype(v.dtype), v)

    o = lax.map(block, qb)
    o = o.transpose(1, 0, 2, 3, 4).reshape(B, S, N_HEADS, V_DIM)
    of = o.astype(jnp.float32)
    of = of * lax.rsqrt(jnp.mean(jnp.square(of), axis=-1, keepdims=True) + RMS_EPS)
    of = of * subln_g.astype(jnp.float32) * (1.0 - lambda_init)
    return of.reshape(B, S, ATTN_WIDTH).astype(v.dtype)


def centred_dwconv(x, w, b):
    S = x.shape[1]
    xp = jnp.pad(x, ((0, 0), (CONV_LEFT, CONV_WIDTH - 1 - CONV_LEFT), (0, 0)))
    y = b
    for tap in range(CONV_WIDTH):
        y = y + w[tap] * xp[:, tap:tap + S]
    return y


def block_diag(x, w, b):
    B, S = x.shape[0], x.shape[1]
    xb = x.reshape(B, S, LRU_BLOCKS, LRU_BLOCK)
    return jnp.einsum('bsnd,nde->bsne', xb, w).reshape(B, S, LRU_WIDTH) + b


def rg_lru(x, w_gates, b_gates, lam, reverse):
    r = jax.nn.sigmoid(block_diag(x, w_gates[0], b_gates[0]).astype(jnp.float32))
    i = jax.nn.sigmoid(block_diag(x, w_gates[1], b_gates[1]).astype(jnp.float32))
    log_a = -LRU_C * r * jax.nn.softplus(-lam.astype(jnp.float32))
    a = jnp.exp(log_a)
    drive = jnp.sqrt(-jnp.expm1(2.0 * log_a)) * (i * x.astype(jnp.float32))

    def combine(left, right):
        a1, b1 = left
        a2, b2 = right
        return a1 * a2, a2 * b1 + b2

    _, h = lax.associative_scan(combine, (a, drive), axis=1, reverse=reverse)
    return h


def encoder_layer(x, c, cos, sin, lambda_init, w_ada, b_ada, w_in, lambda_q1, lambda_k1, lambda_q2,
                  lambda_k2, subln_g, conv_w, conv_b, w_lru_gates, b_lru_gates, lru_lambda,
                  w_attn_branch, w_lru_branch, w_out, ln1_g, ln1_b, w_ffn_in, w_ffn_out, ln2_g, ln2_b):
    B, S = x.shape[0], x.shape[1]
    mod = jax.nn.silu(c) @ w_ada + b_ada
    sh1, sc1, g1, sh2, sc2, g2 = [m[:, None, :] for m in jnp.split(mod, 6, axis=-1)]

    h = layer_norm(x) * (1.0 + sc1) + sh1
    proj = h @ w_in
    q, k, v, xr, yr, ga, gl = jnp.split(proj, IN_SPLITS, axis=-1)
    q = apply_rope(q.reshape(B, S, 2 * N_HEADS, HEAD_DIM), cos, sin)
    k = apply_rope(k.reshape(B, S, 2 * N_HEADS, HEAD_DIM), cos, sin)
    v = v.reshape(B, S, N_HEADS, V_DIM)
    lam = (jnp.exp(jnp.sum(lambda_q1.astype(jnp.float32) * lambda_k1.astype(jnp.float32)))
           - jnp.exp(jnp.sum(lambda_q2.astype(jnp.float32) * lambda_k2.astype(jnp.float32)))
           + lambda_init)
    attn = diff_attention(q, k, v, lam, lambda_init, subln_g)

    xc = centred_dwconv(xr, conv_w, conv_b)
    rec = (rg_lru(xc, w_lru_gates[0], b_lru_gates[0], lru_lambda[0], False)
           + rg_lru(xc, w_lru_gates[1], b_lru_gates[1], lru_lambda[1], True))
    rec = (rec.astype(x.dtype) * jax.nn.gelu(yr))

    merged = jax.nn.sigmoid(ga) * (attn @ w_attn_branch) + jax.nn.sigmoid(gl) * (rec @ w_lru_branch)
    x = layer_norm(ALPHA * x + g1 * (merged @ w_out), ln1_g, ln1_b)

    h = layer_norm(x) * (1.0 + sc2) + sh2
    gate, up = jnp.split(h @ w_ffn_in, 2, axis=-1)
    f = (jax.nn.silu(gate) * up) @ w_ffn_out
    x = layer_norm(ALPHA * x + g2 * f, ln2_g, ln2_b)
    return x


def run_trunk(x, c, w_ada, b_ada, w_in, lambda_q1, lambda_k1, lambda_q2, lambda_k2, subln_g,
              conv_w, conv_b, w_lru_gates, b_lru_gates, lru_lambda, w_attn_branch, w_lru_branch,
              w_out, ln1_g, ln1_b, w_ffn_in, w_ffn_out, ln2_g, ln2_b):
    cos, sin = rope_tables(x.shape[1])
    for l in range(DEPTH):
        lambda_init = 0.8 - 0.6 * math.exp(-0.3 * l)
        x = encoder_layer(x, c, cos, sin, lambda_init, w_ada[l], b_ada[l], w_in[l], lambda_q1[l],
                          lambda_k1[l], lambda_q2[l], lambda_k2[l], subln_g[l], conv_w[l], conv_b[l],
                          w_lru_gates[l], b_lru_gates[l], lru_lambda[l], w_attn_branch[l],
                          w_lru_branch[l], w_out[l], ln1_g[l], ln1_b[l], w_ffn_in[l], w_ffn_out[l],
                          ln2_g[l], ln2_b[l])
    return x


def setup_inputs(seed: int = 0) -> dict:
    key = jax.random.key(seed)
    ks = jax.random.split(key, 32)
    f32 = jnp.float32
    nrm = lambda k, shape, s: jax.random.normal(k, shape, f32) * s
    u = jax.random.uniform(ks[20], (DEPTH, 2, LRU_WIDTH), f32, minval=0.9, maxval=0.999)
    a = u ** (1.0 / LRU_C)
    lru_lambda = jnp.log(a) - jnp.log1p(-a)
    return {
        'x_prompt': nrm(ks[0], (BATCH, SEQ, D_MODEL), 1.0),
        'x_sample': nrm(ks[1], (DEC_BATCH, DEC_SEQ, D_MODEL), 1.0),
        'c_prompt': nrm(ks[2], (BATCH, D_MODEL), 1.0),
        'c_sample': nrm(ks[3], (DEC_BATCH, D_MODEL), 1.0),
        'w_ada': nrm(ks[4], (DEPTH, D_MODEL, 6 * D_MODEL), D_MODEL ** -0.5 * 0.5),
        'b_ada': nrm(ks[5], (DEPTH, 6 * D_MODEL), 0.02),
        'w_in': nrm(ks[6], (DEPTH, D_MODEL, IN_WIDTH), D_MODEL ** -0.5),
        'lambda_q1': nrm(ks[7], (DEPTH, HEAD_DIM), 0.1),
        'lambda_k1': nrm(ks[8], (DEPTH, HEAD_DIM), 0.1),
        'lambda_q2': nrm(ks[9], (DEPTH, HEAD_DIM), 0.1),
        'lambda_k2': nrm(ks[10], (DEPTH, HEAD_DIM), 0.1),
        'subln_g': 1.0 + nrm(ks[11], (DEPTH, V_DIM), 0.02),
        'conv_w': nrm(ks[12], (DEPTH, CONV_WIDTH, LRU_WIDTH), CONV_WIDTH ** -0.5),
        'conv_b': nrm(ks[13], (DEPTH, LRU_WIDTH), 0.02),
        'w_lru_gates': nrm(ks[14], (DEPTH, 2, 2, LRU_BLOCKS, LRU_BLOCK, LRU_BLOCK), LRU_BLOCK ** -0.5),
        'b_lru_gates': nrm(ks[15], (DEPTH, 2, 2, LRU_WIDTH), 0.02),
        'lru_lambda': lru_lambda,
        'w_attn_branch': nrm(ks[16], (DEPTH, ATTN_WIDTH, D_MODEL), ATTN_WIDTH ** -0.5),
        'w_lru_branch': nrm(ks[17], (DEPTH, LRU_WIDTH, D_MODEL), LRU_WIDTH ** -0.5),
        'w_out': nrm(ks[18], (DEPTH, D_MODEL, D_MODEL), D_MODEL ** -0.5 * BETA),
        'ln1_g': 1.0 + nrm(ks[19], (DEPTH, D_MODEL), 0.02),
        'ln1_b': nrm(ks[21], (DEPTH, D_MODEL), 0.02),
        'w_ffn_in': nrm(ks[22], (DEPTH, D_MODEL, 2 * D_FF), D_MODEL ** -0.5),
        'w_ffn_out': nrm(ks[23], (DEPTH, D_FF, D_MODEL), D_FF ** -0.5 * BETA),
        'ln2_g': 1.0 + nrm(ks[24], (DEPTH, D_MODEL), 0.02),
        'ln2_b': nrm(ks[25], (DEPTH, D_MODEL), 0.02),
    }


def reference(x_prompt, x_sample, c_prompt, c_sample, w_ada, b_ada, w_in, lambda_q1, lambda_k1,
              lambda_q2, lambda_k2, subln_g, conv_w, conv_b, w_lru_gates, b_lru_gates, lru_lambda,
              w_attn_branch, w_lru_branch, w_out, ln1_g, ln1_b, w_ffn_in, w_ffn_out, ln2_g, ln2_b):
    y_prompt = run_trunk(x_prompt, c_prompt, w_ada, b_ada, w_in, lambda_q1, lambda_k1, lambda_q2,
                         lambda_k2, subln_g, conv_w, conv_b, w_lru_gates, b_lru_gates, lru_lambda,
                         w_attn_branch, w_lru_branch, w_out, ln1_g, ln1_b, w_ffn_in, w_ffn_out,
                         ln2_g, ln2_b)
    y_sample = run_trunk(x_sample, c_sample, w_ada, b_ada, w_in, lambda_q1, lambda_k1, lambda_q2,
                         lambda_k2, subln_g, conv_w, conv_b, w_lru_gates, b_lru_gates, lru_lambda,
                         w_attn_branch, w_lru_branch, w_out, ln1_g, ln1_b, w_ffn_in, w_ffn_out,
                         ln2_g, ln2_b)
    return (y_prompt, y_sample)
```

```python
import functools
import math

import jax
import jax.numpy as jnp
from jax import lax
from jax.experimental import pallas as pl
from jax.experimental.pallas import tpu as pltpu

D_MODEL = 1024
N_HEADS = 8
HEAD_DIM = 64
V_DIM = 2 * HEAD_DIM
ROPE_THETA = 10000.0
RMS_EPS = 1e-5
LRU_WIDTH = 1024
LRU_BLOCKS = 16
LRU_BLOCK = LRU_WIDTH // LRU_BLOCKS
LRU_C = 8.0
CONV_WIDTH = 4
CONV_LEFT = 2
D_FF = 2816
DEPTH = 1
ALPHA = (2.0 * DEPTH) ** 0.25
LN_EPS = 1e-5
LOG2E = 1.4426950408889634

LANES = 128
SUBLANES = 8
VMEM_LIMIT = 56 * 1024 * 1024

F32 = jnp.float32
BF16 = jnp.bfloat16


def _cparams(n_axes):
    return pltpu.CompilerParams(
        dimension_semantics=("arbitrary",) * n_axes,
        vmem_limit_bytes=VMEM_LIMIT,
    )


def _resident(shape):
    nd = len(shape)
    return pl.BlockSpec(shape, lambda *_: (0,) * nd, pipeline_mode=pl.Buffered(1))


def _layer_norm(x):
    mu = jnp.mean(x, axis=-1, keepdims=True)
    xc = x - mu
    var = jnp.mean(xc * xc, axis=-1, keepdims=True)
    return xc * lax.rsqrt(var + LN_EPS)


def _sigmoid(x):
    return 0.5 * jnp.tanh(0.5 * x) + 0.5


def _split_bf16(x):
    hi = x.astype(BF16)
    lo = (x - hi.astype(F32)).astype(BF16)
    return hi, lo


def _ada_kernel(c_ref, w_ref, b_ref, o_ref):
    c = c_ref[...]
    s_hi, s_lo = _split_bf16(c * _sigmoid(c))
    w_hi, w_lo = _split_bf16(w_ref[...])
    acc = jnp.dot(s_hi, w_hi, preferred_element_type=F32)
    acc += jnp.dot(s_lo, w_hi, preferred_element_type=F32)
    acc += jnp.dot(s_hi, w_lo, preferred_element_type=F32)
    o_ref[...] = acc + b_ref[...]


def _ada(c, w_ada, b_ada):
    rows = c.shape[0]
    n_out = w_ada.shape[1]
    tn = 1024
    return pl.pallas_call(
        _ada_kernel,
        grid=(n_out // tn,),
        in_specs=[
            pl.BlockSpec((rows, D_MODEL), lambda j: (0, 0)),
            pl.BlockSpec((D_MODEL, tn), lambda j: (0, j)),
            pl.BlockSpec((1, tn), lambda j: (0, j)),
        ],
        out_specs=pl.BlockSpec((rows, tn), lambda j: (0, j)),
        out_shape=jax.ShapeDtypeStruct((rows, n_out), F32),
        compiler_params=_cparams(1),
        name="ada",
    )(c, w_ada, b_ada.reshape(1, n_out))


N_PROJ = 7


def _inproj_kernel(x_ref, mod_ref, cos_ref, sin_ref, w_ref, *out_refs):
    h = _layer_norm(x_ref[...]) * (1.0 + mod_ref[1:2, :]) + mod_ref[0:1, :]
    hb = h.astype(BF16)
    cos = cos_ref[...]
    sin = sin_ref[...]
    q_scale = HEAD_DIM ** -0.5 * LOG2E
    for g, o_ref in enumerate(out_refs):
        p = jnp.dot(hb, w_ref[:, g * D_MODEL:(g + 1) * D_MODEL], preferred_element_type=F32)
        if g < 2:
            for hh in range(N_HEADS):
                t = p[:, hh * LANES:(hh + 1) * LANES]
                r = t * cos + pltpu.roll(t, LANES // 2, axis=1) * sin
                if g == 0:
                    r = r * q_scale
                o_ref[:, hh * LANES:(hh + 1) * LANES] = r.astype(o_ref.dtype)
        else:
            o_ref[...] = p.astype(o_ref.dtype)


def _inproj(x, mod, cos_t, sin_t, w_in_b):
    B, S, _ = x.shape
    tm = min(512, S)
    tok = lambda: pl.BlockSpec((None, tm, D_MODEL), lambda b, i: (b, i, 0))
    return pl.pallas_call(
        _inproj_kernel,
        grid=(B, S // tm),
        in_specs=[
            tok(),
            pl.BlockSpec((None, 6, D_MODEL), lambda b, i: (b, 0, 0)),
            pl.BlockSpec((tm, LANES), lambda b, i: (i, 0)),
            pl.BlockSpec((tm, LANES), lambda b, i: (i, 0)),
            _resident(w_in_b.shape),
        ],
        out_specs=[tok() for _ in range(N_PROJ)],
        out_shape=[jax.ShapeDtypeStruct((B, S, D_MODEL), BF16) for _ in range(N_PROJ)],
        compiler_params=_cparams(2),
        name="inproj",
    )(x, mod, cos_t, sin_t, w_in_b)


def _attn_kernel(lam_ref, g_ref, q_ref, k_ref, v_ref, o_ref, vt_ref, s_ref, *, kc, lambda_init):
    S = k_ref.shape[0]
    tq = q_ref.shape[0]
    nc = S // kc

    @pl.when(pl.program_id(2) == 0)
    def _():
        vt_ref[...] = v_ref[...].astype(F32).T.astype(BF16)

    lv = lam_ref[...]
    lam = (jnp.exp(jnp.sum(lv[0:1] * lv[1:2], axis=1, keepdims=True))
           - jnp.exp(jnp.sum(lv[2:3] * lv[3:4], axis=1, keepdims=True)) + lambda_init)

    q = q_ref[...]
    lane = lax.broadcasted_iota(jnp.int32, q.shape, 1)
    is_map0 = (lane % (2 * 32)) < 32
    zero = jnp.zeros_like(q)
    qm = jnp.concatenate([jnp.where(is_map0, q, zero), jnp.where(is_map0, zero, q)], axis=0)

    def scores(c, m):
        off = pl.multiple_of(c * kc, kc)
        s = lax.dot_general(k_ref[pl.ds(off, kc), :], qm, (((1,), (1,)), ((), ())),
                            preferred_element_type=F32)
        s_ref[pl.ds(off, kc), :] = s
        return jnp.maximum(m, jnp.max(s, axis=0, keepdims=True))

    m = lax.fori_loop(0, nc, scores, jnp.full((1, 2 * tq), -jnp.inf, F32))

    def accumulate(c, carry):
        l, ot = carry
        off = pl.multiple_of(c * kc, kc)
        e = jnp.exp2(s_ref[pl.ds(off, kc), :] - m)
        l = l + jnp.sum(e, axis=0, keepdims=True)
        ot = ot + jnp.dot(vt_ref[:, pl.ds(off, kc)], e.astype(BF16), preferred_element_type=F32)
        return l, ot

    l, ot = lax.fori_loop(0, nc, accumulate,
                          (jnp.zeros((1, 2 * tq), F32), jnp.zeros((V_DIM, 2 * tq), F32)))
    ot = ot * (1.0 / l)
    o = ot[:, :tq] - lam * ot[:, tq:]
    o = o * lax.rsqrt(jnp.mean(o * o, axis=0, keepdims=True) + RMS_EPS)
    o = o * g_ref[...] * (1.0 - lambda_init)
    o_ref[...] = o.T.astype(o_ref.dtype)


def _attention(q, k, v, lam_vecs, subln_g, lambda_init):
    B, S, _ = q.shape
    tq = min(256, S)
    kc = min(512, S)
    kv = lambda: pl.BlockSpec((None, S, LANES), lambda b, h, i: (b, 0, h))
    return pl.pallas_call(
        functools.partial(_attn_kernel, kc=kc, lambda_init=lambda_init),
        grid=(B, N_HEADS, S // tq),
        in_specs=[
            pl.BlockSpec((4, HEAD_DIM), lambda b, h, i: (0, 0)),
            pl.BlockSpec((V_DIM, 1), lambda b, h, i: (0, 0)),
            pl.BlockSpec((None, tq, LANES), lambda b, h, i: (b, i, h)),
            kv(),
            kv(),
        ],
        out_specs=pl.BlockSpec((None, tq, LANES), lambda b, h, i: (b, i, h)),
        out_shape=jax.ShapeDtypeStruct((B, S, D_MODEL), BF16),
        scratch_shapes=[
            pltpu.VMEM((V_DIM, S), BF16),
            pltpu.VMEM((S, 2 * tq), F32),
        ],
        compiler_params=_cparams(3),
        name="attn",
    )(lam_vecs, subln_g.reshape(V_DIM, 1), q, k, v)


LRU_TS = 256
SCAN_PAD = 4


def _lru_kernel(xr_ref, yr_ref, cw_ref, cb_ref, wg_ref, bg_ref, lam_ref, o_ref,
                x_s, af_s, df_s, ab_s, db_s, hf_s, pf_s, hb_s, pb_s, r_s):
    S = xr_ref.shape[0]
    pitch = S // SUBLANES + SCAN_PAD
    n_ts = S // LRU_TS
    pad_rows = SUBLANES * pitch - S

    x_s[0:SUBLANES, :] = jnp.zeros((SUBLANES, LANES), F32)
    x_s[S + SUBLANES:S + 2 * SUBLANES, :] = jnp.zeros((SUBLANES, LANES), F32)

    def stage(i, _):
        off = pl.multiple_of(i * LRU_TS, LRU_TS)
        x_s[pl.ds(off + SUBLANES, LRU_TS), :] = xr_ref[pl.ds(off, LRU_TS), :].astype(F32)
        return 0

    lax.fori_loop(0, n_ts, stage, 0)

    cw = cw_ref[...]
    cb = cb_ref[...]
    wg = wg_ref[...]
    bg = bg_ref[...]
    z = -lam_ref[...]
    softplus = jnp.maximum(z, 0.0) + jnp.log1p(jnp.exp(-jnp.abs(z)))
    neg_c_sp = -LRU_C * softplus

    def gates(i, _):
        off = pl.multiple_of(i * LRU_TS, LRU_TS)
        xc = cb
        for tap in range(CONV_WIDTH):
            xc = xc + cw[tap:tap + 1, :] * x_s[pl.ds(off + SUBLANES + tap - CONV_LEFT, LRU_TS), :]
        g = jnp.dot(xc.astype(BF16), wg, preferred_element_type=F32) + bg
        for d, (a_s, d_s) in enumerate(((af_s, df_s), (ab_s, db_s))):
            r = _sigmoid(g[:, (2 * d) * LANES:(2 * d + 1) * LANES])
            ig = _sigmoid(g[:, (2 * d + 1) * LANES:(2 * d + 2) * LANES])
            log_a = neg_c_sp[d:d + 1, :] * r
            a_s[pl.ds(off, LRU_TS), :] = jnp.exp(log_a)
            th = jnp.tanh(log_a)
            d_s[pl.ds(off, LRU_TS), :] = jnp.sqrt(-2.0 * th / (1.0 - th)) * (ig * xc)
        return 0

    lax.fori_loop(0, n_ts, gates, 0)

    for ref in (af_s, df_s, ab_s, db_s):
        ref[S:S + pad_rows, :] = jnp.zeros((pad_rows, LANES), F32)

    zeros = jnp.zeros((SUBLANES, LANES), F32)
    ones = jnp.ones((SUBLANES, LANES), F32)

    def fwd(j, carry):
        h, p = carry
        a = af_s[pl.ds(j, SUBLANES, stride=pitch), :]
        d = df_s[pl.ds(j, SUBLANES, stride=pitch), :]
        h = a * h + d
        p = a * p
        hf_s[j] = h
        pf_s[j] = p
        return h, p

    hf_last, pf_last = lax.fori_loop(0, pitch, fwd, (zeros, ones))

    def bwd(jj, carry):
        h, p = carry
        j = pitch - 1 - jj
        a = ab_s[pl.ds(j, SUBLANES, stride=pitch), :]
        d = db_s[pl.ds(j, SUBLANES, stride=pitch), :]
        h = a * h + d
        p = a * p
        hb_s[j] = h
        pb_s[j] = p
        return h, p

    hb_first, pb_first = lax.fori_loop(0, pitch, bwd, (zeros, ones))

    row = lax.broadcasted_iota(jnp.int32, (SUBLANES, LANES), 0)

    def pick(v, s):
        return jnp.sum(jnp.where(row == s, v, 0.0), axis=0, keepdims=True)

    cf = zeros
    for s in range(1, SUBLANES):
        nxt = pick(hf_last, s - 1) + pick(pf_last, s - 1) * pick(cf, s - 1)
        cf = jnp.where(row == s, nxt, cf)
    cbk = zeros
    for s in range(SUBLANES - 2, -1, -1):
        nxt = pick(hb_first, s + 1) + pick(pb_first, s + 1) * pick(cbk, s + 1)
        cbk = jnp.where(row == s, nxt, cbk)

    def fix(j, _):
        r_s[pl.ds(j, SUBLANES, stride=pitch), :] = (hf_s[j] + pf_s[j] * cf) + (hb_s[j] + pb_s[j] * cbk)
        return 0

    lax.fori_loop(0, pitch, fix, 0)

    def emit(i, _):
        off = pl.multiple_of(i * LRU_TS, LRU_TS)
        y = yr_ref[pl.ds(off, LRU_TS), :].astype(F32)
        o_ref[pl.ds(off, LRU_TS), :] = (r_s[pl.ds(off, LRU_TS), :] * jax.nn.gelu(y)).astype(o_ref.dtype)
        return 0

    lax.fori_loop(0, n_ts, emit, 0)


def _lru(xr, yr, conv_w, conv_b, wg, bg, lru_lambda):
    B, S, _ = xr.shape
    ncb = LRU_WIDTH // LANES
    pitch = S // SUBLANES + SCAN_PAD
    col = lambda: pl.BlockSpec((None, S, LANES), lambda b, c: (b, 0, c))
    seq = lambda: pltpu.VMEM((SUBLANES * pitch, LANES), F32)
    packed = lambda: pltpu.VMEM((pitch, SUBLANES, LANES), F32)
    return pl.pallas_call(
        _lru_kernel,
        grid=(B, ncb),
        in_specs=[
            col(),
            col(),
            pl.BlockSpec((CONV_WIDTH, LANES), lambda b, c: (0, c)),
            pl.BlockSpec((1, LANES), lambda b, c: (0, c)),
            pl.BlockSpec((None, LANES, 4 * LANES), lambda b, c: (c, 0, 0)),
            pl.BlockSpec((None, 1, 4 * LANES), lambda b, c: (c, 0, 0)),
            pl.BlockSpec((2, LANES), lambda b, c: (0, c)),
        ],
        out_specs=col(),
        out_shape=jax.ShapeDtypeStruct((B, S, LRU_WIDTH), BF16),
        scratch_shapes=[
            pltpu.VMEM((S + 2 * SUBLANES, LANES), F32),
            seq(), seq(), seq(), seq(),
            packed(), packed(), packed(), packed(),
            seq(),
        ],
        compiler_params=_cparams(2),
        name="lru",
    )(xr, yr, conv_w, conv_b.reshape(1, LRU_WIDTH), wg, bg, lru_lambda)


def _pack_gate_weights(w_gates, b_gates):
    ncb = LRU_WIDTH // LANES
    per = LANES // LRU_BLOCK
    w = w_gates.reshape(4, ncb, per, LRU_BLOCK, LRU_BLOCK)
    eye = jnp.eye(per, dtype=w.dtype)
    dense = jnp.einsum("gcpde,pq->cpdgqe", w, eye)
    dense = dense.reshape(ncb, LANES, 4 * LANES)
    bias = b_gates.reshape(4, ncb, LANES).transpose(1, 0, 2).reshape(ncb, 1, 4 * LANES)
    return dense.astype(BF16), bias


def _out_kernel(x_ref, attn_ref, rec_ref, ga_ref, gl_ref, mod_ref, wab_ref, wlb_ref, wo_ref,
                wfi_ref, wfo_ref, ln_ref, y_ref):
    a = jnp.dot(attn_ref[...], wab_ref[...], preferred_element_type=F32)
    r = jnp.dot(rec_ref[...], wlb_ref[...], preferred_element_type=F32)
    merged = _sigmoid(ga_ref[...].astype(F32)) * a + _sigmoid(gl_ref[...].astype(F32)) * r
    mix = jnp.dot(merged.astype(BF16), wo_ref[...], preferred_element_type=F32)
    x1 = _layer_norm(ALPHA * x_ref[...] + mod_ref[2:3, :] * mix) * ln_ref[0:1, :] + ln_ref[1:2, :]

    h = _layer_norm(x1) * (1.0 + mod_ref[4:5, :]) + mod_ref[3:4, :]
    gu = jnp.dot(h.astype(BF16), wfi_ref[...], preferred_element_type=F32)
    gate = gu[:, :D_FF]
    up = gu[:, D_FF:]
    act = (gate * _sigmoid(gate) * up).astype(BF16)
    f = jnp.dot(act, wfo_ref[...], preferred_element_type=F32)
    y = _layer_norm(ALPHA * x1 + mod_ref[5:6, :] * f) * ln_ref[2:3, :] + ln_ref[3:4, :]
    y_ref[...] = y


def _out(x, attn, rec, ga, gl, mod, wab, wlb, wo, wfi, wfo, ln):
    B, S, _ = x.shape
    tm = min(256, S)
    tok = lambda: pl.BlockSpec((None, tm, D_MODEL), lambda b, i: (b, i, 0))
    return pl.pallas_call(
        _out_kernel,
        grid=(B, S // tm),
        in_specs=[
            tok(), tok(), tok(), tok(), tok(),
            pl.BlockSpec((None, 6, D_MODEL), lambda b, i: (b, 0, 0)),
            _resident(wab.shape), _resident(wlb.shape), _resident(wo.shape),
            _resident(wfi.shape), _resident(wfo.shape),
            _resident(ln.shape),
        ],
        out_specs=tok(),
        out_shape=jax.ShapeDtypeStruct((B, S, D_MODEL), F32),
        compiler_params=_cparams(2),
        name="out",
    )(x, attn, rec, ga, gl, mod, wab, wlb, wo, wfi, wfo, ln)


def _rope_tables(seq):
    inv = 1.0 / (ROPE_THETA ** (jnp.arange(0, HEAD_DIM, 2, dtype=F32) / HEAD_DIM))
    ang = jnp.arange(seq, dtype=F32)[:, None] * inv[None, :]
    cos, sin = jnp.cos(ang), jnp.sin(ang)
    return (jnp.concatenate([cos, cos, cos, cos], axis=1),
            jnp.concatenate([-sin, -sin, sin, sin], axis=1))


def _permute_qk_columns(w):
    d = w.shape[0]
    return w.reshape(d, N_HEADS, 2, 2, HEAD_DIM // 2).transpose(0, 1, 3, 2, 4).reshape(d, N_HEADS * LANES)


def kernel(x_prompt, x_sample, c_prompt, c_sample, w_ada, b_ada, w_in, lambda_q1, lambda_k1, lambda_q2, lambda_k2, subln_g, conv_w, conv_b, w_lru_gates, b_lru_gates, lru_lambda, w_attn_branch, w_lru_branch, w_out, ln1_g, ln1_b, w_ffn_in, w_ffn_out, ln2_g, ln2_b):
    assert w_ada.shape[0] == DEPTH == 1
    lambda_init = 0.8 - 0.6 * math.exp(-0.3 * 0)

    w_in0 = w_in[0]
    w_in_b = jnp.concatenate(
        [_permute_qk_columns(w_in0[:, :D_MODEL]), _permute_qk_columns(w_in0[:, D_MODEL:2 * D_MODEL]),
         w_in0[:, 2 * D_MODEL:]], axis=1).astype(BF16)
    lam_vecs = jnp.concatenate([lambda_q1, lambda_k1, lambda_q2, lambda_k2], axis=0).astype(F32)
    wg, bg = _pack_gate_weights(w_lru_gates[0], b_lru_gates[0])
    wab = w_attn_branch[0].astype(BF16)
    wlb = w_lru_branch[0].astype(BF16)
    wo = w_out[0].astype(BF16)
    wfi = w_ffn_in[0].astype(BF16)
    wfo = w_ffn_out[0].astype(BF16)
    ln = jnp.concatenate([ln1_g, ln1_b, ln2_g, ln2_b], axis=0).astype(F32)

    nb = c_prompt.shape[0]
    mod_all = _ada(jnp.concatenate([c_prompt, c_sample], axis=0), w_ada[0], b_ada[0])
    mod_all = mod_all.reshape(mod_all.shape[0], 6, D_MODEL)

    def trunk(x, mod):
        cos_t, sin_t = _rope_tables(x.shape[1])
        q, k, v, xr, yr, ga, gl = _inproj(x, mod, cos_t, sin_t, w_in_b)
        attn = _attention(q, k, v, lam_vecs, subln_g[0].astype(F32), lambda_init)
        rec = _lru(xr, yr, conv_w[0], conv_b[0], wg, bg, lru_lambda[0])
        return _out(x, attn, rec, ga, gl, mod, wab, wlb, wo, wfi, wfo, ln)

    return (trunk(x_prompt, mod_all[:nb]), trunk(x_sample, mod_all[nb:]))
```

```python
import functools
import math

import jax
import jax.numpy as jnp
from jax import lax
from jax.experimental import pallas as pl
from jax.experimental.pallas import tpu as pltpu

D_MODEL = 1024
N_HEADS = 8
HEAD_DIM = 64
V_DIM = 2 * HEAD_DIM
ROPE_THETA = 10000.0
RMS_EPS = 1e-5
LRU_WIDTH = 1024
LRU_BLOCKS = 16
LRU_BLOCK = LRU_WIDTH // LRU_BLOCKS
LRU_C = 8.0
CONV_WIDTH = 4
CONV_LEFT = 2
D_FF = 2816
DEPTH = 1
ALPHA = (2.0 * DEPTH) ** 0.25
LN_EPS = 1e-5
LOG2E = 1.4426950408889634

LANES = 128
SUBLANES = 8
VMEM_LIMIT = 56 * 1024 * 1024

F32 = jnp.float32
BF16 = jnp.bfloat16


def _cparams(n_axes):
    return pltpu.CompilerParams(
        dimension_semantics=("arbitrary",) * n_axes,
        vmem_limit_bytes=VMEM_LIMIT,
    )


def _resident(shape):
    nd = len(shape)
    return pl.BlockSpec(shape, lambda *_: (0,) * nd, pipeline_mode=pl.Buffered(1))


def _layer_norm(x):
    mu = jnp.mean(x, axis=-1, keepdims=True)
    xc = x - mu
    var = jnp.mean(xc * xc, axis=-1, keepdims=True)
    return xc * lax.rsqrt(var + LN_EPS)


def _sigmoid(x):
    return 0.5 * jnp.tanh(0.5 * x) + 0.5


def _split_bf16(x):
    hi = x.astype(BF16)
    lo = (x - hi.astype(F32)).astype(BF16)
    return hi, lo


def _ada_kernel(c_ref, w_ref, b_ref, o_ref):
    c = c_ref[...]
    s_hi, s_lo = _split_bf16(c * _sigmoid(c))
    w_hi, w_lo = _split_bf16(w_ref[...])
    acc = jnp.dot(s_hi, w_hi, preferred_element_type=F32)
    acc += jnp.dot(s_lo, w_hi, preferred_element_type=F32)
    acc += jnp.dot(s_hi, w_lo, preferred_element_type=F32)
    o_ref[...] = acc + b_ref[...]


def _ada(c, w_ada, b_ada):
    rows = c.shape[0]
    n_out = w_ada.shape[1]
    tn = 1024
    return pl.pallas_call(
        _ada_kernel,
        grid=(n_out // tn,),
        in_specs=[
            pl.BlockSpec((rows, D_MODEL), lambda j: (0, 0)),
            pl.BlockSpec((D_MODEL, tn), lambda j: (0, j)),
            pl.BlockSpec((1, tn), lambda j: (0, j)),
        ],
        out_specs=pl.BlockSpec((rows, tn), lambda j: (0, j)),
        out_shape=jax.ShapeDtypeStruct((rows, n_out), F32),
        compiler_params=_cparams(1),
        name="ada",
    )(c, w_ada, b_ada.reshape(1, n_out))


N_PROJ = 7


def _inproj_kernel(x_ref, mod_ref, cos_ref, sin_ref, w_ref, *out_refs):
    h = _layer_norm(x_ref[...]) * (1.0 + mod_ref[1:2, :]) + mod_ref[0:1, :]
    hb = h.astype(BF16)
    cos = cos_ref[...]
    sin = sin_ref[...]
    q_scale = HEAD_DIM ** -0.5 * LOG2E
    for g, o_ref in enumerate(out_refs):
        p = jnp.dot(hb, w_ref[:, g * D_MODEL:(g + 1) * D_MODEL], preferred_element_type=F32)
        if g < 2:
            for hh in range(N_HEADS):
                t = p[:, hh * LANES:(hh + 1) * LANES]
                r = t * cos + pltpu.roll(t, LANES // 2, axis=1) * sin
                if g == 0:
                    r = r * q_scale
                o_ref[:, hh * LANES:(hh + 1) * LANES] = r.astype(o_ref.dtype)
        else:
            o_ref[...] = p.astype(o_ref.dtype)


def _inproj(x, mod, cos_t, sin_t, w_in_b):
    B, S, _ = x.shape
    tm = min(512, S)
    tok = lambda: pl.BlockSpec((None, tm, D_MODEL), lambda b, i: (b, i, 0))
    return pl.pallas_call(
        _inproj_kernel,
        grid=(B, S // tm),
        in_specs=[
            tok(),
            pl.BlockSpec((None, 6, D_MODEL), lambda b, i: (b, 0, 0)),
            pl.BlockSpec((tm, LANES), lambda b, i: (i, 0)),
            pl.BlockSpec((tm, LANES), lambda b, i: (i, 0)),
            _resident(w_in_b.shape),
        ],
        out_specs=[tok() for _ in range(N_PROJ)],
        out_shape=[jax.ShapeDtypeStruct((B, S, D_MODEL), BF16) for _ in range(N_PROJ)],
        compiler_params=_cparams(2),
        name="inproj",
    )(x, mod, cos_t, sin_t, w_in_b)


def _attn_kernel(lam_ref, g_ref, q_ref, k_ref, v_ref, o_ref, vt_ref, *, kc, lambda_init):
    S = k_ref.shape[0]
    tq = q_ref.shape[0]
    nc = S // kc

    @pl.when(pl.program_id(2) == 0)
    def _():
        vt_ref[...] = v_ref[...].astype(F32).T.astype(BF16)

    lv = lam_ref[...]
    lam = (jnp.exp(jnp.sum(lv[0:1] * lv[1:2], axis=1, keepdims=True))
           - jnp.exp(jnp.sum(lv[2:3] * lv[3:4], axis=1, keepdims=True)) + lambda_init)

    q = q_ref[...]
    lane = lax.broadcasted_iota(jnp.int32, q.shape, 1)
    is_map0 = (lane % (2 * 32)) < 32
    zero = jnp.zeros_like(q)
    qm = jnp.concatenate([jnp.where(is_map0, q, zero), jnp.where(is_map0, zero, q)], axis=0)

    def scores(c):
        return lax.dot_general(k_ref[c * kc:(c + 1) * kc, :], qm, (((1,), (1,)), ((), ())),
                               preferred_element_type=F32)

    m = l = ot = None
    s_next = scores(0)
    for c in range(nc):
        s = s_next
        if c + 1 < nc:
            s_next = scores(c + 1)
        m_c = jnp.max(s, axis=0, keepdims=True)
        m_new = m_c if c == 0 else jnp.maximum(m, m_c)
        e = jnp.exp2(s - m_new)
        l_c = jnp.sum(e, axis=0, keepdims=True)
        ot_c = jnp.dot(vt_ref[:, c * kc:(c + 1) * kc], e.astype(BF16), preferred_element_type=F32)
        if c == 0:
            l, ot = l_c, ot_c
        else:
            alpha = jnp.exp2(m - m_new)
            l = alpha * l + l_c
            ot = alpha * ot + ot_c
        m = m_new
    ot = ot * (1.0 / l)
    o = ot[:, :tq] - lam * ot[:, tq:]
    o = o * lax.rsqrt(jnp.mean(o * o, axis=0, keepdims=True) + RMS_EPS)
    o = o * g_ref[...] * (1.0 - lambda_init)
    o_ref[...] = o.T.astype(o_ref.dtype)


def _attention(q, k, v, lam_vecs, subln_g, lambda_init):
    B, S, _ = q.shape
    tq = min(256, S)
    kc = min(512, S)
    kv = lambda: pl.BlockSpec((None, S, LANES), lambda b, h, i: (b, 0, h))
    return pl.pallas_call(
        functools.partial(_attn_kernel, kc=kc, lambda_init=lambda_init),
        grid=(B, N_HEADS, S // tq),
        in_specs=[
            pl.BlockSpec((4, HEAD_DIM), lambda b, h, i: (0, 0)),
            pl.BlockSpec((V_DIM, 1), lambda b, h, i: (0, 0)),
            pl.BlockSpec((None, tq, LANES), lambda b, h, i: (b, i, h)),
            kv(),
            kv(),
        ],
        out_specs=pl.BlockSpec((None, tq, LANES), lambda b, h, i: (b, i, h)),
        out_shape=jax.ShapeDtypeStruct((B, S, D_MODEL), BF16),
        scratch_shapes=[
            pltpu.VMEM((V_DIM, S), BF16),
        ],
        compiler_params=_cparams(3),
        name="attn",
    )(lam_vecs, subln_g.reshape(V_DIM, 1), q, k, v)


LRU_TS = 256
SCAN_PAD = 4
SCAN_UNROLL = 4


def _lru_kernel(xr_ref, yr_ref, cw_ref, cb_ref, wg_ref, bg_ref, lam_ref, o_ref,
                x_s, af_s, df_s, ab_s, db_s):
    S = xr_ref.shape[0]
    pitch = S // SUBLANES + SCAN_PAD
    n_ts = S // LRU_TS
    pad_rows = SUBLANES * pitch - S

    x_s[0:SUBLANES, :] = jnp.zeros((SUBLANES, LANES), F32)
    x_s[S + SUBLANES:S + 2 * SUBLANES, :] = jnp.zeros((SUBLANES, LANES), F32)

    def stage(i, _):
        off = pl.multiple_of(i * LRU_TS, LRU_TS)
        x_s[pl.ds(off + SUBLANES, LRU_TS), :] = xr_ref[pl.ds(off, LRU_TS), :].astype(F32)
        return 0

    lax.fori_loop(0, n_ts, stage, 0)

    cw = cw_ref[...]
    cb = cb_ref[...]
    wg = wg_ref[...]
    bg = bg_ref[...]
    z = -lam_ref[...]
    softplus = jnp.maximum(z, 0.0) + jnp.log1p(jnp.exp(-jnp.abs(z)))
    neg_c_sp = -LRU_C * softplus

    def gates(i, _):
        off = pl.multiple_of(i * LRU_TS, LRU_TS)
        xc = cb
        for tap in range(CONV_WIDTH):
            xc = xc + cw[tap:tap + 1, :] * x_s[pl.ds(off + SUBLANES + tap - CONV_LEFT, LRU_TS), :]
        g = jnp.dot(xc.astype(BF16), wg, preferred_element_type=F32) + bg
        for d, (a_s, d_s) in enumerate(((af_s, df_s), (ab_s, db_s))):
            r = _sigmoid(g[:, (2 * d) * LANES:(2 * d + 1) * LANES])
            ig = _sigmoid(g[:, (2 * d + 1) * LANES:(2 * d + 2) * LANES])
            log_a = neg_c_sp[d:d + 1, :] * r
            a_s[pl.ds(off, LRU_TS), :] = jnp.exp(log_a)
            th = jnp.tanh(log_a)
            d_s[pl.ds(off, LRU_TS), :] = jnp.sqrt(-2.0 * th / (1.0 - th)) * (ig * xc)
        return 0

    lax.fori_loop(0, n_ts, gates, 0)

    for ref in (af_s, df_s, ab_s, db_s):
        ref[S:S + pad_rows, :] = jnp.zeros((pad_rows, LANES), F32)

    zeros = jnp.zeros((SUBLANES, LANES), F32)
    ones = jnp.ones((SUBLANES, LANES), F32)

    def rows(j):
        return pl.ds(j, SUBLANES, stride=pitch)

    def local_scan(j, carry):
        hf, pf, hb, pb = carry
        jb = pitch - 1 - j
        a = af_s[rows(j), :]
        hf = a * hf + df_s[rows(j), :]
        pf = a * pf
        a = ab_s[rows(jb), :]
        hb = a * hb + db_s[rows(jb), :]
        pb = a * pb
        return hf, pf, hb, pb

    hf_last, pf_last, hb_first, pb_first = lax.fori_loop(
        0, pitch, local_scan, (zeros, ones, zeros, ones), unroll=SCAN_UNROLL)

    row = lax.broadcasted_iota(jnp.int32, (SUBLANES, LANES), 0)

    def pick(v, s):
        return jnp.sum(jnp.where(row == s, v, 0.0), axis=0, keepdims=True)

    cf = zeros
    for s in range(1, SUBLANES):
        nxt = pick(hf_last, s - 1) + pick(pf_last, s - 1) * pick(cf, s - 1)
        cf = jnp.where(row == s, nxt, cf)
    cbk = zeros
    for s in range(SUBLANES - 2, -1, -1):
        nxt = pick(hb_first, s + 1) + pick(pb_first, s + 1) * pick(cbk, s + 1)
        cbk = jnp.where(row == s, nxt, cbk)

    def full_scan(j, carry):
        hf, hb = carry
        jb = pitch - 1 - j
        hf = af_s[rows(j), :] * hf + df_s[rows(j), :]
        df_s[rows(j), :] = hf
        hb = ab_s[rows(jb), :] * hb + db_s[rows(jb), :]
        db_s[rows(jb), :] = hb
        return hf, hb

    lax.fori_loop(0, pitch, full_scan, (cf, cbk), unroll=SCAN_UNROLL)

    def emit(i, _):
        off = pl.multiple_of(i * LRU_TS, LRU_TS)
        y = yr_ref[pl.ds(off, LRU_TS), :].astype(F32)
        rec = df_s[pl.ds(off, LRU_TS), :] + db_s[pl.ds(off, LRU_TS), :]
        o_ref[pl.ds(off, LRU_TS), :] = (rec * jax.nn.gelu(y)).astype(o_ref.dtype)
        return 0

    lax.fori_loop(0, n_ts, emit, 0)


def _lru(xr, yr, conv_w, conv_b, wg, bg, lru_lambda):
    B, S, _ = xr.shape
    ncb = LRU_WIDTH // LANES
    pitch = S // SUBLANES + SCAN_PAD
    col = lambda: pl.BlockSpec((None, S, LANES), lambda b, c: (b, 0, c))
    assert pitch % SCAN_UNROLL == 0
    seq = lambda: pltpu.VMEM((SUBLANES * pitch, LANES), F32)
    return pl.pallas_call(
        _lru_kernel,
        grid=(B, ncb),
        in_specs=[
            col(),
            col(),
            pl.BlockSpec((CONV_WIDTH, LANES), lambda b, c: (0, c)),
            pl.BlockSpec((1, LANES), lambda b, c: (0, c)),
            pl.BlockSpec((None, LANES, 4 * LANES), lambda b, c: (c, 0, 0)),
            pl.BlockSpec((None, 1, 4 * LANES), lambda b, c: (c, 0, 0)),
            pl.BlockSpec((2, LANES), lambda b, c: (0, c)),
        ],
        out_specs=col(),
        out_shape=jax.ShapeDtypeStruct((B, S, LRU_WIDTH), BF16),
        scratch_shapes=[
            pltpu.VMEM((S + 2 * SUBLANES, LANES), F32),
            seq(), seq(), seq(), seq(),
        ],
        compiler_params=_cparams(2),
        name="lru",
    )(xr, yr, conv_w, conv_b.reshape(1, LRU_WIDTH), wg, bg, lru_lambda)


def _pack_gate_weights(w_gates, b_gates):
    ncb = LRU_WIDTH // LANES
    per = LANES // LRU_BLOCK
    w = w_gates.reshape(4, ncb, per, LRU_BLOCK, LRU_BLOCK)
    eye = jnp.eye(per, dtype=w.dtype)
    dense = jnp.einsum("gcpde,pq->cpdgqe", w, eye)
    dense = dense.reshape(ncb, LANES, 4 * LANES)
    bias = b_gates.reshape(4, ncb, LANES).transpose(1, 0, 2).reshape(ncb, 1, 4 * LANES)
    return dense.astype(BF16), bias


def _out_kernel(x_ref, attn_ref, rec_ref, ga_ref, gl_ref, mod_ref, wab_ref, wlb_ref, wo_ref,
                wfi_ref, wfo_ref, ln_ref, y_ref):
    a = jnp.dot(attn_ref[...], wab_ref[...], preferred_element_type=F32)
    r = jnp.dot(rec_ref[...], wlb_ref[...], preferred_element_type=F32)
    merged = _sigmoid(ga_ref[...].astype(F32)) * a + _sigmoid(gl_ref[...].astype(F32)) * r
    mix = jnp.dot(merged.astype(BF16), wo_ref[...], preferred_element_type=F32)
    x1 = _layer_norm(ALPHA * x_ref[...] + mod_ref[2:3, :] * mix) * ln_ref[0:1, :] + ln_ref[1:2, :]

    h = _layer_norm(x1) * (1.0 + mod_ref[4:5, :]) + mod_ref[3:4, :]
    gu = jnp.dot(h.astype(BF16), wfi_ref[...], preferred_element_type=F32)
    gate = gu[:, :D_FF]
    up = gu[:, D_FF:]
    act = (gate * _sigmoid(gate) * up).astype(BF16)
    f = jnp.dot(act, wfo_ref[...], preferred_element_type=F32)
    y = _layer_norm(ALPHA * x1 + mod_ref[5:6, :] * f) * ln_ref[2:3, :] + ln_ref[3:4, :]
    y_ref[...] = y


def _out(x, attn, rec, ga, gl, mod, wab, wlb, wo, wfi, wfo, ln):
    B, S, _ = x.shape
    tm = min(256, S)
    tok = lambda: pl.BlockSpec((None, tm, D_MODEL), lambda b, i: (b, i, 0))
    return pl.pallas_call(
        _out_kernel,
        grid=(B, S // tm),
        in_specs=[
            tok(), tok(), tok(), tok(), tok(),
            pl.BlockSpec((None, 6, D_MODEL), lambda b, i: (b, 0, 0)),
            _resident(wab.shape), _resident(wlb.shape), _resident(wo.shape),
            _resident(wfi.shape), _resident(wfo.shape),
            _resident(ln.shape),
        ],
        out_specs=tok(),
        out_shape=jax.ShapeDtypeStruct((B, S, D_MODEL), F32),
        compiler_params=_cparams(2),
        name="out",
    )(x, attn, rec, ga, gl, mod, wab, wlb, wo, wfi, wfo, ln)


def _rope_tables(seq):
    inv = 1.0 / (ROPE_THETA ** (jnp.arange(0, HEAD_DIM, 2, dtype=F32) / HEAD_DIM))
    ang = jnp.arange(seq, dtype=F32)[:, None] * inv[None, :]
    cos, sin = jnp.cos(ang), jnp.sin(ang)
    return (jnp.concatenate([cos, cos, cos, cos], axis=1),
            jnp.concatenate([-sin, -sin, sin, sin], axis=1))


def _permute_qk_columns(w):
    d = w.shape[0]
    return w.reshape(d, N_HEADS, 2, 2, HEAD_DIM // 2).transpose(0, 1, 3, 2, 4).reshape(d, N_HEADS * LANES)


def kernel(x_prompt, x_sample, c_prompt, c_sample, w_ada, b_ada, w_in, lambda_q1, lambda_k1, lambda_q2, lambda_k2, subln_g, conv_w, conv_b, w_lru_gates, b_lru_gates, lru_lambda, w_attn_branch, w_lru_branch, w_out, ln1_g, ln1_b, w_ffn_in, w_ffn_out, ln2_g, ln2_b):
    assert w_ada.shape[0] == DEPTH == 1
    lambda_init = 0.8 - 0.6 * math.exp(-0.3 * 0)

    w_in0 = w_in[0]
    w_in_b = jnp.concatenate(
        [_permute_qk_columns(w_in0[:, :D_MODEL]), _permute_qk_columns(w_in0[:, D_MODEL:2 * D_MODEL]),
         w_in0[:, 2 * D_MODEL:]], axis=1).astype(BF16)
    lam_vecs = jnp.concatenate([lambda_q1, lambda_k1, lambda_q2, lambda_k2], axis=0).astype(F32)
    wg, bg = _pack_gate_weights(w_lru_gates[0], b_lru_gates[0])
    wab = w_attn_branch[0].astype(BF16)
    wlb = w_lru_branch[0].astype(BF16)
    wo = w_out[0].astype(BF16)
    wfi = w_ffn_in[0].astype(BF16)
    wfo = w_ffn_out[0].astype(BF16)
    ln = jnp.concatenate([ln1_g, ln1_b, ln2_g, ln2_b], axis=0).astype(F32)

    nb = c_prompt.shape[0]
    mod_all = _ada(jnp.concatenate([c_prompt, c_sample], axis=0), w_ada[0], b_ada[0])
    mod_all = mod_all.reshape(mod_all.shape[0], 6, D_MODEL)

    def trunk(x, mod):
        cos_t, sin_t = _rope_tables(x.shape[1])
        q, k, v, xr, yr, ga, gl = _inproj(x, mod, cos_t, sin_t, w_in_b)
        attn = _attention(q, k, v, lam_vecs, subln_g[0].astype(F32), lambda_init)
        rec = _lru(xr, yr, conv_w[0], conv_b[0], wg, bg, lru_lambda[0])
        return _out(x, attn, rec, ga, gl, mod, wab, wlb, wo, wfi, wfo, ln)

    return (trunk(x_prompt, mod_all[:nb]), trunk(x_sample, mod_all[nb:]))
```

```python
import functools
import math

import jax
import jax.numpy as jnp
from jax import lax
from jax.experimental import pallas as pl
from jax.experimental.pallas import tpu as pltpu

D_MODEL = 1024
N_HEADS = 8
HEAD_DIM = 64
V_DIM = 2 * HEAD_DIM
ROPE_THETA = 10000.0
RMS_EPS = 1e-5
LRU_WIDTH = 1024
LRU_BLOCKS = 16
LRU_BLOCK = LRU_WIDTH // LRU_BLOCKS
LRU_C = 8.0
CONV_WIDTH = 4
CONV_LEFT = 2
D_FF = 2816
DEPTH = 1
ALPHA = (2.0 * DEPTH) ** 0.25
LN_EPS = 1e-5
LOG2E = 1.4426950408889634

LANES = 128
SUBLANES = 8
VMEM_LIMIT = 56 * 1024 * 1024

F32 = jnp.float32
BF16 = jnp.bfloat16


def _cparams(n_axes):
    return pltpu.CompilerParams(
        dimension_semantics=("arbitrary",) * n_axes,
        vmem_limit_bytes=VMEM_LIMIT,
    )


def _resident(shape):
    nd = len(shape)
    return pl.BlockSpec(shape, lambda *_: (0,) * nd, pipeline_mode=pl.Buffered(1))


def _layer_norm(x):
    mu = jnp.mean(x, axis=-1, keepdims=True)
    xc = x - mu
    var = jnp.mean(xc * xc, axis=-1, keepdims=True)
    return xc * lax.rsqrt(var + LN_EPS)


def _sigmoid(x):
    return 0.5 * jnp.tanh(0.5 * x) + 0.5


def _split_bf16(x):
    hi = x.astype(BF16)
    lo = (x - hi.astype(F32)).astype(BF16)
    return hi, lo


def _ada_kernel(c_ref, w_ref, b_ref, o_ref):
    c = c_ref[...]
    s_hi, s_lo = _split_bf16(c * _sigmoid(c))
    w_hi, w_lo = _split_bf16(w_ref[...])
    acc = jnp.dot(s_hi, w_hi, preferred_element_type=F32)
    acc += jnp.dot(s_lo, w_hi, preferred_element_type=F32)
    acc += jnp.dot(s_hi, w_lo, preferred_element_type=F32)
    o_ref[...] = acc + b_ref[...]


def _ada(c, w_ada, b_ada):
    rows = c.shape[0]
    n_out = w_ada.shape[1]
    tn = 1024
    return pl.pallas_call(
        _ada_kernel,
        grid=(n_out // tn,),
        in_specs=[
            pl.BlockSpec((rows, D_MODEL), lambda j: (0, 0)),
            pl.BlockSpec((D_MODEL, tn), lambda j: (0, j)),
            pl.BlockSpec((1, tn), lambda j: (0, j)),
        ],
        out_specs=pl.BlockSpec((rows, tn), lambda j: (0, j)),
        out_shape=jax.ShapeDtypeStruct((rows, n_out), F32),
        compiler_params=_cparams(1),
        name="ada",
    )(c, w_ada, b_ada.reshape(1, n_out))


N_PROJ = 7


def _inproj_kernel(x_ref, mod_ref, cos_ref, sin_ref, w_ref, *out_refs):
    h = _layer_norm(x_ref[...]) * (1.0 + mod_ref[1:2, :]) + mod_ref[0:1, :]
    hb = h.astype(BF16)
    cos = cos_ref[...]
    sin = sin_ref[...]
    q_scale = HEAD_DIM ** -0.5 * LOG2E
    for g, o_ref in enumerate(out_refs):
        p = jnp.dot(hb, w_ref[:, g * D_MODEL:(g + 1) * D_MODEL], preferred_element_type=F32)
        if g < 2:
            for hh in range(N_HEADS):
                t = p[:, hh * LANES:(hh + 1) * LANES]
                r = t * cos + pltpu.roll(t, LANES // 2, axis=1) * sin
                if g == 0:
                    r = r * q_scale
                o_ref[:, hh * LANES:(hh + 1) * LANES] = r.astype(o_ref.dtype)
        else:
            o_ref[...] = p.astype(o_ref.dtype)


def _inproj(x, mod, cos_t, sin_t, w_in_b):
    B, S, _ = x.shape
    tm = min(512, S)
    tok = lambda: pl.BlockSpec((None, tm, D_MODEL), lambda b, i: (b, i, 0))
    return pl.pallas_call(
        _inproj_kernel,
        grid=(B, S // tm),
        in_specs=[
            tok(),
            pl.BlockSpec((None, 6, D_MODEL), lambda b, i: (b, 0, 0)),
            pl.BlockSpec((tm, LANES), lambda b, i: (i, 0)),
            pl.BlockSpec((tm, LANES), lambda b, i: (i, 0)),
            _resident(w_in_b.shape),
        ],
        out_specs=[tok() for _ in range(N_PROJ)],
        out_shape=[jax.ShapeDtypeStruct((B, S, D_MODEL), BF16) for _ in range(N_PROJ)],
        compiler_params=_cparams(2),
        name="inproj",
    )(x, mod, cos_t, sin_t, w_in_b)


ONES_ROWS = 16


def _attn_kernel(lam_ref, g_ref, q_ref, k_ref, v_ref, o_ref, vt_ref, s_ref, acc_ref, *, tq, kc, lambda_init):
    S = k_ref.shape[0]
    nq = S // tq
    nc = S // kc

    vt_ref[0:V_DIM, :] = v_ref[...].astype(F32).T.astype(BF16)
    vt_ref[V_DIM:V_DIM + ONES_ROWS, :] = jnp.ones((ONES_ROWS, S), BF16)

    lv = lam_ref[...]
    lam = (jnp.exp(jnp.sum(lv[0:1] * lv[1:2], axis=1, keepdims=True))
           - jnp.exp(jnp.sum(lv[2:3] * lv[3:4], axis=1, keepdims=True)) + lambda_init)

    lane = lax.broadcasted_iota(jnp.int32, (tq, LANES), 1)
    is_map0 = (lane % (2 * 32)) < 32

    def masked_queries(i):
        q = q_ref[pl.ds(pl.multiple_of(i * tq, tq), tq), :]
        zero = jnp.zeros_like(q)
        return jnp.concatenate([jnp.where(is_map0, q, zero), jnp.where(is_map0, zero, q)], axis=0)

    def score_chunk(qm, slot, c, m):
        s = lax.dot_general(k_ref[c * kc:(c + 1) * kc, :], qm, (((1,), (1,)), ((), ())),
                            preferred_element_type=F32)
        s_ref[slot, c * kc:(c + 1) * kc, :] = s
        m_c = jnp.max(s, axis=0, keepdims=True)
        return m_c if m is None else jnp.maximum(m, m_c)

    def value_chunk(slot, c, m):
        e = jnp.exp2(s_ref[slot, c * kc:(c + 1) * kc, :] - m).astype(BF16)
        part = jnp.dot(vt_ref[:, c * kc:(c + 1) * kc], e, preferred_element_type=F32)
        if c == 0:
            acc_ref[...] = part
        else:
            acc_ref[...] += part

    def finish(i):
        acc = acc_ref[...]
        ot = acc[0:V_DIM] * (1.0 / acc[V_DIM:V_DIM + 1])
        o = ot[:, :tq] - lam * ot[:, tq:]
        o = o * lax.rsqrt(jnp.mean(o * o, axis=0, keepdims=True) + RMS_EPS)
        o = o * g_ref[...] * (1.0 - lambda_init)
        o_ref[pl.ds(pl.multiple_of(i * tq, tq), tq), :] = o.T.astype(o_ref.dtype)

    def scores_only(i, slot):
        qm = masked_queries(i)
        m = None
        for c in range(nc):
            m = score_chunk(qm, slot, c, m)
        return m

    def overlapped(i, slot, m):
        qm_next = masked_queries(i + 1)
        m_next = None
        for c in range(nc):
            m_next = score_chunk(qm_next, 1 - slot, c, m_next)
            value_chunk(slot, c, m)
        finish(i)
        return m_next

    def values_only(i, slot, m):
        for c in range(nc):
            value_chunk(slot, c, m)
        finish(i)

    m_even = scores_only(0, 0)
    if nq == 1:
        values_only(0, 0, m_even)
    else:
        def pair(j, m):
            m_odd = overlapped(2 * j, 0, m)
            return overlapped(2 * j + 1, 1, m_odd)

        m_even = lax.fori_loop(0, nq // 2 - 1, pair, m_even)
        m_odd = overlapped(nq - 2, 0, m_even)
        values_only(nq - 1, 1, m_odd)


def _attention(q, k, v, lam_vecs, subln_g, lambda_init):
    B, S, _ = q.shape
    tq = min(256, S)
    kc = min(512, S)
    nq = S // tq
    assert S % tq == 0 and S % kc == 0 and (nq == 1 or nq % 2 == 0)
    col = lambda: pl.BlockSpec((None, S, LANES), lambda b, h: (b, 0, h))
    return pl.pallas_call(
        functools.partial(_attn_kernel, tq=tq, kc=kc, lambda_init=lambda_init),
        grid=(B, N_HEADS),
        in_specs=[
            pl.BlockSpec((4, HEAD_DIM), lambda b, h: (0, 0)),
            pl.BlockSpec((V_DIM, 1), lambda b, h: (0, 0)),
            col(), col(), col(),
        ],
        out_specs=col(),
        out_shape=jax.ShapeDtypeStruct((B, S, D_MODEL), BF16),
        scratch_shapes=[
            pltpu.VMEM((V_DIM + ONES_ROWS, S), BF16),
            pltpu.VMEM((2, S, 2 * tq), F32),
            pltpu.VMEM((V_DIM + ONES_ROWS, 2 * tq), F32),
        ],
        compiler_params=_cparams(2),
        name="attn",
    )(lam_vecs, subln_g.reshape(V_DIM, 1), q, k, v)


LRU_TS = 256
SCAN_PAD = 4
SCAN_UNROLL = 4


def _lru_kernel(xr_ref, yr_ref, cw_ref, cb_ref, wg_ref, bg_ref, lam_ref, o_ref,
                x_s, af_s, df_s, ab_s, db_s, rf_s, rb_s):
    S = xr_ref.shape[0]
    pitch = S // SUBLANES + SCAN_PAD
    n_ts = S // LRU_TS
    pad_rows = SUBLANES * pitch - S

    x_s[0:SUBLANES, :] = jnp.zeros((SUBLANES, LANES), F32)
    x_s[S + SUBLANES:S + 2 * SUBLANES, :] = jnp.zeros((SUBLANES, LANES), F32)

    def stage(i, _):
        off = pl.multiple_of(i * LRU_TS, LRU_TS)
        x_s[pl.ds(off + SUBLANES, LRU_TS), :] = xr_ref[pl.ds(off, LRU_TS), :].astype(F32)
        return 0

    lax.fori_loop(0, n_ts, stage, 0)

    cw = cw_ref[...]
    cb = cb_ref[...]
    wg = wg_ref[...]
    bg = bg_ref[...]
    z = -lam_ref[...]
    softplus = jnp.maximum(z, 0.0) + jnp.log1p(jnp.exp(-jnp.abs(z)))
    half_neg_c_sp = (-0.5 * LRU_C) * softplus

    def gates(i, _):
        off = pl.multiple_of(i * LRU_TS, LRU_TS)
        xc = cb
        for tap in range(CONV_WIDTH):
            xc = xc + cw[tap:tap + 1, :] * x_s[pl.ds(off + SUBLANES + tap - CONV_LEFT, LRU_TS), :]
        g = jnp.dot(xc.astype(BF16), wg, preferred_element_type=F32) + bg
        half_xc = 0.5 * xc
        for d, (a_s, d_s) in enumerate(((af_s, df_s), (ab_s, db_s))):
            k = half_neg_c_sp[d:d + 1, :]
            log_a = k * jnp.tanh(g[:, (2 * d) * LANES:(2 * d + 1) * LANES]) + k
            gated_x = half_xc * jnp.tanh(g[:, (2 * d + 1) * LANES:(2 * d + 2) * LANES]) + half_xc
            a_s[pl.ds(off, LRU_TS), :] = jnp.exp(log_a)
            th = jnp.tanh(log_a)
            z = (-2.0 * th) * (1.0 / (1.0 - th))
            root = jnp.where(z > 0.0, z * lax.rsqrt(z), 0.0)
            d_s[pl.ds(off, LRU_TS), :] = root * gated_x
        return 0

    lax.fori_loop(0, n_ts, gates, 0, unroll=2 if n_ts % 2 == 0 else 1)

    for ref in (af_s, df_s, ab_s, db_s):
        ref[S:S + pad_rows, :] = jnp.zeros((pad_rows, LANES), F32)

    zeros = jnp.zeros((SUBLANES, LANES), F32)
    ones = jnp.ones((SUBLANES, LANES), F32)

    def rows(j):
        return pl.ds(j, SUBLANES, stride=pitch)

    def two_steps(a_s, d_s, j0, j1):
        a0 = a_s[rows(j0), :]
        d0 = d_s[rows(j0), :]
        a1 = a_s[rows(j1), :]
        return a0, d0, a1 * a0, a1 * d0 + d_s[rows(j1), :]

    def local_scan(t, carry):
        hf, pf, hb, pb = carry
        j = 2 * t
        jb = pitch - 1 - j
        _, _, a2, d2 = two_steps(af_s, df_s, j, j + 1)
        hf = a2 * hf + d2
        pf = a2 * pf
        _, _, a2, d2 = two_steps(ab_s, db_s, jb, jb - 1)
        hb = a2 * hb + d2
        pb = a2 * pb
        return hf, pf, hb, pb

    hf_last, pf_last, hb_first, pb_first = lax.fori_loop(
        0, pitch // 2, local_scan, (zeros, ones, zeros, ones), unroll=SCAN_UNROLL // 2)

    row = lax.broadcasted_iota(jnp.int32, (SUBLANES, LANES), 0)

    def pick(v, s):
        return jnp.sum(jnp.where(row == s, v, 0.0), axis=0, keepdims=True)

    cf = zeros
    for s in range(1, SUBLANES):
        nxt = pick(hf_last, s - 1) + pick(pf_last, s - 1) * pick(cf, s - 1)
        cf = jnp.where(row == s, nxt, cf)
    cbk = zeros
    for s in range(SUBLANES - 2, -1, -1):
        nxt = pick(hb_first, s + 1) + pick(pb_first, s + 1) * pick(cbk, s + 1)
        cbk = jnp.where(row == s, nxt, cbk)

    def full_scan(t, carry):
        hf, hb = carry
        j = 2 * t
        jb = pitch - 1 - j
        a0, d0, a2, d2 = two_steps(af_s, df_s, j, j + 1)
        rf_s[rows(j), :] = a0 * hf + d0
        hf = a2 * hf + d2
        rf_s[rows(j + 1), :] = hf
        a0, d0, a2, d2 = two_steps(ab_s, db_s, jb, jb - 1)
        rb_s[rows(jb), :] = a0 * hb + d0
        hb = a2 * hb + d2
        rb_s[rows(jb - 1), :] = hb
        return hf, hb

    lax.fori_loop(0, pitch // 2, full_scan, (cf, cbk), unroll=SCAN_UNROLL // 2)

    def emit(i, _):
        off = pl.multiple_of(i * LRU_TS, LRU_TS)
        y = yr_ref[pl.ds(off, LRU_TS), :].astype(F32)
        rec = rf_s[pl.ds(off, LRU_TS), :] + rb_s[pl.ds(off, LRU_TS), :]
        o_ref[pl.ds(off, LRU_TS), :] = (rec * jax.nn.gelu(y)).astype(o_ref.dtype)
        return 0

    lax.fori_loop(0, n_ts, emit, 0)


def _lru(xr, yr, conv_w, conv_b, wg, bg, lru_lambda):
    B, S, _ = xr.shape
    ncb = LRU_WIDTH // LANES
    pitch = S // SUBLANES + SCAN_PAD
    col = lambda: pl.BlockSpec((None, S, LANES), lambda b, c: (b, 0, c))
    assert pitch % SCAN_UNROLL == 0
    seq = lambda: pltpu.VMEM((SUBLANES * pitch, LANES), F32)
    return pl.pallas_call(
        _lru_kernel,
        grid=(B, ncb),
        in_specs=[
            col(),
            col(),
            pl.BlockSpec((CONV_WIDTH, LANES), lambda b, c: (0, c)),
            pl.BlockSpec((1, LANES), lambda b, c: (0, c)),
            pl.BlockSpec((None, LANES, 4 * LANES), lambda b, c: (c, 0, 0)),
            pl.BlockSpec((None, 1, 4 * LANES), lambda b, c: (c, 0, 0)),
            pl.BlockSpec((2, LANES), lambda b, c: (0, c)),
        ],
        out_specs=col(),
        out_shape=jax.ShapeDtypeStruct((B, S, LRU_WIDTH), BF16),
        scratch_shapes=[
            pltpu.VMEM((S + 2 * SUBLANES, LANES), F32),
            seq(), seq(), seq(), seq(), seq(), seq(),
        ],
        compiler_params=_cparams(2),
        name="lru",
    )(xr, yr, conv_w, conv_b.reshape(1, LRU_WIDTH), wg, bg, lru_lambda)


def _pack_gate_weights(w_gates, b_gates):
    ncb = LRU_WIDTH // LANES
    per = LANES // LRU_BLOCK
    w = w_gates.reshape(4, ncb, per, LRU_BLOCK, LRU_BLOCK)
    eye = jnp.eye(per, dtype=w.dtype)
    dense = jnp.einsum("gcpde,pq->cpdgqe", w, eye)
    dense = dense.reshape(ncb, LANES, 4 * LANES)
    bias = b_gates.reshape(4, ncb, LANES).transpose(1, 0, 2).reshape(ncb, 1, 4 * LANES)
    return (0.5 * dense).astype(BF16), 0.5 * bias


def _out_kernel(x_ref, attn_ref, rec_ref, ga_ref, gl_ref, mod_ref, wab_ref, wlb_ref, wo_ref,
                wfi_ref, wfo_ref, ln_ref, y_ref):
    a = jnp.dot(attn_ref[...], wab_ref[...], preferred_element_type=F32)
    r = jnp.dot(rec_ref[...], wlb_ref[...], preferred_element_type=F32)
    merged = _sigmoid(ga_ref[...].astype(F32)) * a + _sigmoid(gl_ref[...].astype(F32)) * r
    mix = jnp.dot(merged.astype(BF16), wo_ref[...], preferred_element_type=F32)
    x1 = _layer_norm(ALPHA * x_ref[...] + mod_ref[2:3, :] * mix) * ln_ref[0:1, :] + ln_ref[1:2, :]

    h = _layer_norm(x1) * (1.0 + mod_ref[4:5, :]) + mod_ref[3:4, :]
    gu = jnp.dot(h.astype(BF16), wfi_ref[...], preferred_element_type=F32)
    gate = gu[:, :D_FF]
    up = gu[:, D_FF:]
    act = (gate * _sigmoid(gate) * up).astype(BF16)
    f = jnp.dot(act, wfo_ref[...], preferred_element_type=F32)
    y = _layer_norm(ALPHA * x1 + mod_ref[5:6, :] * f) * ln_ref[2:3, :] + ln_ref[3:4, :]
    y_ref[...] = y


def _out(x, attn, rec, ga, gl, mod, wab, wlb, wo, wfi, wfo, ln):
    B, S, _ = x.shape
    tm = min(256, S)
    tok = lambda: pl.BlockSpec((None, tm, D_MODEL), lambda b, i: (b, i, 0))
    return pl.pallas_call(
        _out_kernel,
        grid=(B, S // tm),
        in_specs=[
            tok(), tok(), tok(), tok(), tok(),
            pl.BlockSpec((None, 6, D_MODEL), lambda b, i: (b, 0, 0)),
            _resident(wab.shape), _resident(wlb.shape), _resident(wo.shape),
            _resident(wfi.shape), _resident(wfo.shape),
            _resident(ln.shape),
        ],
        out_specs=tok(),
        out_shape=jax.ShapeDtypeStruct((B, S, D_MODEL), F32),
        compiler_params=_cparams(2),
        name="out",
    )(x, attn, rec, ga, gl, mod, wab, wlb, wo, wfi, wfo, ln)


def _rope_tables(seq):
    inv = 1.0 / (ROPE_THETA ** (jnp.arange(0, HEAD_DIM, 2, dtype=F32) / HEAD_DIM))
    ang = jnp.arange(seq, dtype=F32)[:, None] * inv[None, :]
    cos, sin = jnp.cos(ang), jnp.sin(ang)
    return (jnp.concatenate([cos, cos, cos, cos], axis=1),
            jnp.concatenate([-sin, -sin, sin, sin], axis=1))


def _permute_qk_columns(w):
    d = w.shape[0]
    return w.reshape(d, N_HEADS, 2, 2, HEAD_DIM // 2).transpose(0, 1, 3, 2, 4).reshape(d, N_HEADS * LANES)


def kernel(x_prompt, x_sample, c_prompt, c_sample, w_ada, b_ada, w_in, lambda_q1, lambda_k1, lambda_q2, lambda_k2, subln_g, conv_w, conv_b, w_lru_gates, b_lru_gates, lru_lambda, w_attn_branch, w_lru_branch, w_out, ln1_g, ln1_b, w_ffn_in, w_ffn_out, ln2_g, ln2_b):
    assert w_ada.shape[0] == DEPTH == 1
    lambda_init = 0.8 - 0.6 * math.exp(-0.3 * 0)

    w_in0 = w_in[0]
    w_in_b = jnp.concatenate(
        [_permute_qk_columns(w_in0[:, :D_MODEL]), _permute_qk_columns(w_in0[:, D_MODEL:2 * D_MODEL]),
         w_in0[:, 2 * D_MODEL:]], axis=1).astype(BF16)
    lam_vecs = jnp.concatenate([lambda_q1, lambda_k1, lambda_q2, lambda_k2], axis=0).astype(F32)
    wg, bg = _pack_gate_weights(w_lru_gates[0], b_lru_gates[0])
    wab = w_attn_branch[0].astype(BF16)
    wlb = w_lru_branch[0].astype(BF16)
    wo = w_out[0].astype(BF16)
    wfi = w_ffn_in[0].astype(BF16)
    wfo = w_ffn_out[0].astype(BF16)
    ln = jnp.concatenate([ln1_g, ln1_b, ln2_g, ln2_b], axis=0).astype(F32)

    nb = c_prompt.shape[0]
    mod_all = _ada(jnp.concatenate([c_prompt, c_sample], axis=0), w_ada[0], b_ada[0])
    mod_all = mod_all.reshape(mod_all.shape[0], 6, D_MODEL)

    def trunk(x, mod):
        cos_t, sin_t = _rope_tables(x.shape[1])
        q, k, v, xr, yr, ga, gl = _inproj(x, mod, cos_t, sin_t, w_in_b)
        attn = _attention(q, k, v, lam_vecs, subln_g[0].astype(F32), lambda_init)
        rec = _lru(xr, yr, conv_w[0], conv_b[0], wg, bg, lru_lambda[0])
        return _out(x, attn, rec, ga, gl, mod, wab, wlb, wo, wfi, wfo, ln)

    return (trunk(x_prompt, mod_all[:nb]), trunk(x_sample, mod_all[nb:]))
```

```python
import functools
import math

import jax
import jax.numpy as jnp
from jax import lax
from jax.experimental import pallas as pl
from jax.experimental.pallas import tpu as pltpu

D_MODEL = 1024
N_HEADS = 8
HEAD_DIM = 64
V_DIM = 2 * HEAD_DIM
ROPE_THETA = 10000.0
RMS_EPS = 1e-5
LRU_WIDTH = 1024
LRU_BLOCKS = 16
LRU_BLOCK = LRU_WIDTH // LRU_BLOCKS
LRU_C = 8.0
CONV_WIDTH = 4
CONV_LEFT = 2
D_FF = 2816
DEPTH = 1
ALPHA = (2.0 * DEPTH) ** 0.25
LN_EPS = 1e-5
LOG2E = 1.4426950408889634

LANES = 128
SUBLANES = 8
VMEM_LIMIT = 56 * 1024 * 1024

F32 = jnp.float32
BF16 = jnp.bfloat16


def _cparams(n_axes):
    return pltpu.CompilerParams(
        dimension_semantics=("arbitrary",) * n_axes,
        vmem_limit_bytes=VMEM_LIMIT,
    )


def _resident(shape):
    nd = len(shape)
    return pl.BlockSpec(shape, lambda *_: (0,) * nd, pipeline_mode=pl.Buffered(1))


def _layer_norm(x):
    mu = jnp.mean(x, axis=-1, keepdims=True)
    xc = x - mu
    var = jnp.mean(xc * xc, axis=-1, keepdims=True)
    return xc * lax.rsqrt(var + LN_EPS)


def _sigmoid(x):
    return 0.5 * jnp.tanh(0.5 * x) + 0.5


def _split_bf16(x):
    hi = x.astype(BF16)
    lo = (x - hi.astype(F32)).astype(BF16)
    return hi, lo


def _ada_kernel(c_ref, w_ref, b_ref, o_ref):
    c = c_ref[...]
    s_hi, s_lo = _split_bf16(c * _sigmoid(c))
    w_hi, w_lo = _split_bf16(w_ref[...])
    acc = jnp.dot(s_hi, w_hi, preferred_element_type=F32)
    acc += jnp.dot(s_lo, w_hi, preferred_element_type=F32)
    acc += jnp.dot(s_hi, w_lo, preferred_element_type=F32)
    o_ref[...] = acc + b_ref[...]


def _ada(c, w_ada, b_ada):
    rows = c.shape[0]
    n_out = w_ada.shape[1]
    tn = 1024
    return pl.pallas_call(
        _ada_kernel,
        grid=(n_out // tn,),
        in_specs=[
            pl.BlockSpec((rows, D_MODEL), lambda j: (0, 0)),
            pl.BlockSpec((D_MODEL, tn), lambda j: (0, j)),
            pl.BlockSpec((1, tn), lambda j: (0, j)),
        ],
        out_specs=pl.BlockSpec((rows, tn), lambda j: (0, j)),
        out_shape=jax.ShapeDtypeStruct((rows, n_out), F32),
        compiler_params=_cparams(1),
        name="ada",
    )(c, w_ada, b_ada.reshape(1, n_out))


N_PROJ = 7
IN_ROWS = 256


def _inproj_kernel(x_ref, mod_ref, cos_ref, sin_ref, w_ref, *out_refs):
    tm = x_ref.shape[0]
    groups = [slice(r * IN_ROWS, (r + 1) * IN_ROWS) for r in range(tm // IN_ROWS)]
    hbs = []
    for rows in groups:
        h = _layer_norm(x_ref[rows, :]) * (1.0 + mod_ref[1:2, :]) + mod_ref[0:1, :]
        hbs.append(h.astype(BF16))
    q_scale = HEAD_DIM ** -0.5 * LOG2E
    for g, o_ref in enumerate(out_refs):
        for rows, hb in zip(groups, hbs):
            p = jnp.dot(hb, w_ref[:, g * D_MODEL:(g + 1) * D_MODEL], preferred_element_type=F32)
            if g < 2:
                cos = cos_ref[rows, :]
                sin = sin_ref[rows, :]
                for hh in range(N_HEADS):
                    t = p[:, hh * LANES:(hh + 1) * LANES]
                    r = t * cos + pltpu.roll(t, LANES // 2, axis=1) * sin
                    if g == 0:
                        r = r * q_scale
                    o_ref[rows, hh * LANES:(hh + 1) * LANES] = r.astype(o_ref.dtype)
            else:
                o_ref[rows, :] = p.astype(o_ref.dtype)


def _inproj(x, mod, cos_t, sin_t, w_in_b):
    B, S, _ = x.shape
    tm = min(2 * IN_ROWS, S)
    assert S % tm == 0 and tm % IN_ROWS == 0
    tok = lambda: pl.BlockSpec((None, tm, D_MODEL), lambda b, i: (b, i, 0))
    return pl.pallas_call(
        _inproj_kernel,
        grid=(B, S // tm),
        in_specs=[
            tok(),
            pl.BlockSpec((None, 6, D_MODEL), lambda b, i: (b, 0, 0)),
            pl.BlockSpec((tm, LANES), lambda b, i: (i, 0)),
            pl.BlockSpec((tm, LANES), lambda b, i: (i, 0)),
            _resident(w_in_b.shape),
        ],
        out_specs=[tok() for _ in range(N_PROJ)],
        out_shape=[jax.ShapeDtypeStruct((B, S, D_MODEL), BF16) for _ in range(N_PROJ)],
        compiler_params=_cparams(2),
        name="inproj",
    )(x, mod, cos_t, sin_t, w_in_b)


ONES_ROWS = 16


def _attn_kernel(lam_ref, g_ref, q_ref, k_ref, v_ref, o_ref, vt_ref, s_ref, acc_ref, *, tq, kc, lambda_init):
    S = k_ref.shape[0]
    nq = S // tq
    nc = S // kc

    vt_ref[0:V_DIM, :] = v_ref[...].astype(F32).T.astype(BF16)
    vt_ref[V_DIM:V_DIM + ONES_ROWS, :] = jnp.ones((ONES_ROWS, S), BF16)

    lv = lam_ref[...]
    lam = (jnp.exp(jnp.sum(lv[0:1] * lv[1:2], axis=1, keepdims=True))
           - jnp.exp(jnp.sum(lv[2:3] * lv[3:4], axis=1, keepdims=True)) + lambda_init)

    lane = lax.broadcasted_iota(jnp.int32, (tq, LANES), 1)
    is_map0 = (lane % (2 * 32)) < 32

    def masked_queries(i):
        q = q_ref[pl.ds(pl.multiple_of(i * tq, tq), tq), :]
        zero = jnp.zeros_like(q)
        return jnp.concatenate([jnp.where(is_map0, q, zero), jnp.where(is_map0, zero, q)], axis=0)

    def score_chunk(qm, slot, c, m):
        s = lax.dot_general(k_ref[c * kc:(c + 1) * kc, :], qm, (((1,), (1,)), ((), ())),
                            preferred_element_type=F32)
        s_ref[slot, c * kc:(c + 1) * kc, :] = s
        m_c = jnp.max(s, axis=0, keepdims=True)
        return m_c if m is None else jnp.maximum(m, m_c)

    def value_chunk(slot, c, m):
        e = jnp.exp2(s_ref[slot, c * kc:(c + 1) * kc, :] - m).astype(BF16)
        part = jnp.dot(vt_ref[:, c * kc:(c + 1) * kc], e, preferred_element_type=F32)
        if c == 0:
            acc_ref[...] = part
        else:
            acc_ref[...] += part

    def finish(i):
        acc = acc_ref[...]
        ot = acc[0:V_DIM] * (1.0 / acc[V_DIM:V_DIM + 1])
        o = ot[:, :tq] - lam * ot[:, tq:]
        o = o * lax.rsqrt(jnp.mean(o * o, axis=0, keepdims=True) + RMS_EPS)
        o = o * g_ref[...] * (1.0 - lambda_init)
        o_ref[pl.ds(pl.multiple_of(i * tq, tq), tq), :] = o.T.astype(o_ref.dtype)

    def scores_only(i, slot):
        qm = masked_queries(i)
        m = None
        for c in range(nc):
            m = score_chunk(qm, slot, c, m)
        return m

    def overlapped(i, slot, m):
        qm_next = masked_queries(i + 1)
        m_next = None
        for c in range(nc):
            m_next = score_chunk(qm_next, 1 - slot, c, m_next)
            value_chunk(slot, c, m)
        finish(i)
        return m_next

    def values_only(i, slot, m):
        for c in range(nc):
            value_chunk(slot, c, m)
        finish(i)

    m_even = scores_only(0, 0)
    if nq == 1:
        values_only(0, 0, m_even)
    else:
        def pair(j, m):
            m_odd = overlapped(2 * j, 0, m)
            return overlapped(2 * j + 1, 1, m_odd)

        m_even = lax.fori_loop(0, nq // 2 - 1, pair, m_even)
        m_odd = overlapped(nq - 2, 0, m_even)
        values_only(nq - 1, 1, m_odd)


def _attention(q, k, v, lam_vecs, subln_g, lambda_init):
    B, S, _ = q.shape
    tq = min(256, S)
    kc = min(512, S)
    nq = S // tq
    assert S % tq == 0 and S % kc == 0 and (nq == 1 or nq % 2 == 0)
    col = lambda: pl.BlockSpec((None, S, LANES), lambda b, h: (b, 0, h))
    return pl.pallas_call(
        functools.partial(_attn_kernel, tq=tq, kc=kc, lambda_init=lambda_init),
        grid=(B, N_HEADS),
        in_specs=[
            pl.BlockSpec((4, HEAD_DIM), lambda b, h: (0, 0)),
            pl.BlockSpec((V_DIM, 1), lambda b, h: (0, 0)),
            col(), col(), col(),
        ],
        out_specs=col(),
        out_shape=jax.ShapeDtypeStruct((B, S, D_MODEL), BF16),
        scratch_shapes=[
            pltpu.VMEM((V_DIM + ONES_ROWS, S), BF16),
            pltpu.VMEM((2, S, 2 * tq), F32),
            pltpu.VMEM((V_DIM + ONES_ROWS, 2 * tq), F32),
        ],
        compiler_params=_cparams(2),
        name="attn",
    )(lam_vecs, subln_g.reshape(V_DIM, 1), q, k, v)


LRU_TS = 256
SCAN_GROUP = 8


def _lru_kernel(xr_ref, yr_ref, cw_ref, cb_ref, wg_ref, lam_ref, o_ref,
                x_s, af_s, df_s, ab_s, db_s, rf_s, rb_s, *, ts):
    S = xr_ref.shape[0]
    chunk = S // SUBLANES
    n_ts = S // ts

    def interleaved(i):
        first = i * ts
        return pl.ds((first % chunk) * SUBLANES + first // chunk, ts, stride=SUBLANES)

    x_s[0:SUBLANES, :] = jnp.zeros((SUBLANES, LANES), F32)
    x_s[S + SUBLANES:S + 2 * SUBLANES, :] = jnp.zeros((SUBLANES, LANES), F32)

    def stage(i, _):
        off = pl.multiple_of(i * ts, ts)
        x_s[pl.ds(off + SUBLANES, ts), :] = xr_ref[pl.ds(off, ts), :].astype(F32)
        return 0

    lax.fori_loop(0, n_ts, stage, 0)

    cw = cw_ref[...]
    cb = cb_ref[...]
    wg = wg_ref[...]
    z = -lam_ref[...]
    softplus = jnp.maximum(z, 0.0) + jnp.log1p(jnp.exp(-jnp.abs(z)))
    half_neg_c_sp = (-0.5 * LRU_C) * softplus
    lane = lax.broadcasted_iota(jnp.int32, (ts, LANES), 1)
    bias_lanes = jnp.where(lane < 2, 1.0, 0.0).astype(BF16)
    tiny = float(jnp.finfo(F32).tiny)

    def gates(i, _):
        off = pl.multiple_of(i * ts, ts)
        xc = cb
        for tap in range(CONV_WIDTH):
            xc = xc + cw[tap:tap + 1, :] * x_s[pl.ds(off + SUBLANES + tap - CONV_LEFT, ts), :]
        lhs = jnp.concatenate([xc.astype(BF16), bias_lanes], axis=1)
        g = jnp.dot(lhs, wg, preferred_element_type=F32)
        sx = -math.sqrt(0.5) * xc
        for d, (a_s, d_s) in enumerate(((af_s, df_s), (ab_s, db_s))):
            k = half_neg_c_sp[d:d + 1, :]
            log_a = k * jnp.tanh(g[:, (2 * d) * LANES:(2 * d + 1) * LANES]) + k
            a_s[interleaved(i), :] = jnp.exp(log_a)
            gated_x = sx * jnp.tanh(g[:, (2 * d + 1) * LANES:(2 * d + 2) * LANES]) + sx
            th = jnp.tanh(log_a)
            root = th * lax.rsqrt(jnp.maximum(th * (th - 1.0), tiny))
            d_s[interleaved(i), :] = root * gated_x
        return 0

    lax.fori_loop(0, n_ts, gates, 0, unroll=math.gcd(n_ts, 4))

    zeros = jnp.zeros((SUBLANES, LANES), F32)
    ones = jnp.ones((SUBLANES, LANES), F32)

    n_groups = chunk // SCAN_GROUP
    group_rows = SCAN_GROUP * SUBLANES

    def group(g):
        return pl.ds(pl.multiple_of(g * group_rows, group_rows), group_rows)

    def scan_group(a, d, order, h, p):
        states = {}
        for k0, k1 in zip(order[0::2], order[1::2]):
            a0 = a[k0 * SUBLANES:(k0 + 1) * SUBLANES]
            d0 = d[k0 * SUBLANES:(k0 + 1) * SUBLANES]
            a1 = a[k1 * SUBLANES:(k1 + 1) * SUBLANES]
            a2 = a1 * a0
            d2 = a1 * d0 + d[k1 * SUBLANES:(k1 + 1) * SUBLANES]
            states[k0] = a0 * h + d0
            h = a2 * h + d2
            states[k1] = h
            p = a2 * p
        return h, p, states

    forward = list(range(SCAN_GROUP))
    backward = forward[::-1]

    def local_scan(g, carry):
        hf, pf, hb, pb = carry
        gb = n_groups - 1 - g
        hf, pf, _ = scan_group(af_s[group(g), :], df_s[group(g), :], forward, hf, pf)
        hb, pb, _ = scan_group(ab_s[group(gb), :], db_s[group(gb), :], backward, hb, pb)
        return hf, pf, hb, pb

    hf_last, pf_last, hb_first, pb_first = lax.fori_loop(0, n_groups, local_scan, (zeros, ones, zeros, ones))

    row = lax.broadcasted_iota(jnp.int32, (SUBLANES, LANES), 0)

    def pick(v, s):
        return jnp.sum(jnp.where(row == s, v, 0.0), axis=0, keepdims=True)

    cf = zeros
    for s in range(1, SUBLANES):
        nxt = pick(hf_last, s - 1) + pick(pf_last, s - 1) * pick(cf, s - 1)
        cf = jnp.where(row == s, nxt, cf)
    cbk = zeros
    for s in range(SUBLANES - 2, -1, -1):
        nxt = pick(hb_first, s + 1) + pick(pb_first, s + 1) * pick(cbk, s + 1)
        cbk = jnp.where(row == s, nxt, cbk)

    def full_scan(g, carry):
        hf, hb = carry
        gb = n_groups - 1 - g
        hf, _, states = scan_group(af_s[group(g), :], df_s[group(g), :], forward, hf, ones)
        rf_s[group(g), :] = jnp.concatenate([states[k] for k in forward], axis=0)
        hb, _, states = scan_group(ab_s[group(gb), :], db_s[group(gb), :], backward, hb, ones)
        rb_s[group(gb), :] = jnp.concatenate([states[k] for k in forward], axis=0)
        return hf, hb

    lax.fori_loop(0, n_groups, full_scan, (cf, cbk))

    def emit(i, _):
        off = pl.multiple_of(i * ts, ts)
        y = yr_ref[pl.ds(off, ts), :].astype(F32)
        rec = rf_s[interleaved(i), :] + rb_s[interleaved(i), :]
        o_ref[pl.ds(off, ts), :] = (rec * jax.nn.gelu(y)).astype(o_ref.dtype)
        return 0

    lax.fori_loop(0, n_ts, emit, 0)


def _lru(xr, yr, conv_w, conv_b, wg, lru_lambda):
    B, S, _ = xr.shape
    ncb = LRU_WIDTH // LANES
    chunk = S // SUBLANES
    ts = min(LRU_TS, chunk)
    assert S % SUBLANES == 0 and chunk % ts == 0 and chunk % SCAN_GROUP == 0
    col = lambda: pl.BlockSpec((None, S, LANES), lambda b, c: (b, 0, c))
    seq = lambda: pltpu.VMEM((S, LANES), F32)
    return pl.pallas_call(
        functools.partial(_lru_kernel, ts=ts),
        grid=(B, ncb),
        in_specs=[
            col(),
            col(),
            pl.BlockSpec((CONV_WIDTH, LANES), lambda b, c: (0, c)),
            pl.BlockSpec((1, LANES), lambda b, c: (0, c)),
            pl.BlockSpec((None, 2 * LANES, 4 * LANES), lambda b, c: (c, 0, 0)),
            pl.BlockSpec((2, LANES), lambda b, c: (0, c)),
        ],
        out_specs=col(),
        out_shape=jax.ShapeDtypeStruct((B, S, LRU_WIDTH), BF16),
        scratch_shapes=[
            pltpu.VMEM((S + 2 * SUBLANES, LANES), F32),
            seq(), seq(), seq(), seq(), seq(), seq(),
        ],
        compiler_params=_cparams(2),
        name="lru",
    )(xr, yr, conv_w, conv_b.reshape(1, LRU_WIDTH), wg, lru_lambda)


def _pack_gate_weights(w_gates, b_gates):
    ncb = LRU_WIDTH // LANES
    per = LANES // LRU_BLOCK
    w = w_gates.reshape(4, ncb, per, LRU_BLOCK, LRU_BLOCK)
    eye = jnp.eye(per, dtype=w.dtype)
    dense = jnp.einsum("gcpde,pq->cpdgqe", w, eye)
    dense = (0.5 * dense.reshape(ncb, LANES, 4 * LANES)).astype(BF16)
    bias = 0.5 * b_gates.reshape(4, ncb, LANES).transpose(1, 0, 2).reshape(ncb, 1, 4 * LANES)
    bias_hi, bias_lo = _split_bf16(bias)
    pad = jnp.zeros((ncb, LANES - 2, 4 * LANES), BF16)
    return jnp.concatenate([dense, bias_hi, bias_lo, pad], axis=1)


OUT_ROWS = 256


def _out_kernel(x_ref, attn_ref, rec_ref, ga_ref, gl_ref, mod_ref, wab_ref, wlb_ref, wo_ref,
                wfi_ref, wfo_ref, ln_ref, y_ref):
    tm = x_ref.shape[0]
    groups = [slice(g * OUT_ROWS, (g + 1) * OUT_ROWS) for g in range(tm // OUT_ROWS)]

    def branches(rows):
        a = jnp.dot(attn_ref[rows, :], wab_ref[...], preferred_element_type=F32)
        r = jnp.dot(rec_ref[rows, :], wlb_ref[...], preferred_element_type=F32)
        merged = _sigmoid(ga_ref[rows, :].astype(F32)) * a + _sigmoid(gl_ref[rows, :].astype(F32)) * r
        return merged.astype(BF16)

    def mixer_norm(rows, merged):
        mix = jnp.dot(merged, wo_ref[...], preferred_element_type=F32)
        x1 = _layer_norm(ALPHA * x_ref[rows, :] + mod_ref[2:3, :] * mix) * ln_ref[0:1, :] + ln_ref[1:2, :]
        h = _layer_norm(x1) * (1.0 + mod_ref[4:5, :]) + mod_ref[3:4, :]
        return x1, h.astype(BF16)

    def ffn_hidden(h):
        gu = jnp.dot(h, wfi_ref[...], preferred_element_type=F32)
        gate = gu[:, :D_FF]
        return (gate * _sigmoid(gate) * gu[:, D_FF:]).astype(BF16)

    def ffn_norm(rows, x1, act):
        f = jnp.dot(act, wfo_ref[...], preferred_element_type=F32)
        y_ref[rows, :] = _layer_norm(ALPHA * x1 + mod_ref[5:6, :] * f) * ln_ref[2:3, :] + ln_ref[3:4, :]

    merged = [branches(rows) for rows in groups]
    normed = [mixer_norm(rows, m) for rows, m in zip(groups, merged)]
    acts = [ffn_hidden(h) for _, h in normed]
    for rows, (x1, _), act in zip(groups, normed, acts):
        ffn_norm(rows, x1, act)


def _out(x, attn, rec, ga, gl, mod, wab, wlb, wo, wfi, wfo, ln):
    B, S, _ = x.shape
    tm = min(2 * OUT_ROWS, S)
    assert S % tm == 0 and tm % OUT_ROWS == 0
    tok = lambda: pl.BlockSpec((None, tm, D_MODEL), lambda b, i: (b, i, 0))
    return pl.pallas_call(
        _out_kernel,
        grid=(B, S // tm),
        in_specs=[
            tok(), tok(), tok(), tok(), tok(),
            pl.BlockSpec((None, 6, D_MODEL), lambda b, i: (b, 0, 0)),
            _resident(wab.shape), _resident(wlb.shape), _resident(wo.shape),
            _resident(wfi.shape), _resident(wfo.shape),
            _resident(ln.shape),
        ],
        out_specs=tok(),
        out_shape=jax.ShapeDtypeStruct((B, S, D_MODEL), F32),
        compiler_params=_cparams(2),
        name="out",
    )(x, attn, rec, ga, gl, mod, wab, wlb, wo, wfi, wfo, ln)


def _rope_tables(seq):
    inv = 1.0 / (ROPE_THETA ** (jnp.arange(0, HEAD_DIM, 2, dtype=F32) / HEAD_DIM))
    ang = jnp.arange(seq, dtype=F32)[:, None] * inv[None, :]
    cos, sin = jnp.cos(ang), jnp.sin(ang)
    return (jnp.concatenate([cos, cos, cos, cos], axis=1),
            jnp.concatenate([-sin, -sin, sin, sin], axis=1))


def _permute_qk_columns(w):
    d = w.shape[0]
    return w.reshape(d, N_HEADS, 2, 2, HEAD_DIM // 2).transpose(0, 1, 3, 2, 4).reshape(d, N_HEADS * LANES)


def kernel(x_prompt, x_sample, c_prompt, c_sample, w_ada, b_ada, w_in, lambda_q1, lambda_k1, lambda_q2, lambda_k2, subln_g, conv_w, conv_b, w_lru_gates, b_lru_gates, lru_lambda, w_attn_branch, w_lru_branch, w_out, ln1_g, ln1_b, w_ffn_in, w_ffn_out, ln2_g, ln2_b):
    assert w_ada.shape[0] == DEPTH == 1
    lambda_init = 0.8 - 0.6 * math.exp(-0.3 * 0)

    w_in0 = w_in[0]
    w_in_b = jnp.concatenate(
        [_permute_qk_columns(w_in0[:, :D_MODEL]), _permute_qk_columns(w_in0[:, D_MODEL:2 * D_MODEL]),
         w_in0[:, 2 * D_MODEL:]], axis=1).astype(BF16)
    lam_vecs = jnp.concatenate([lambda_q1, lambda_k1, lambda_q2, lambda_k2], axis=0).astype(F32)
    wg = _pack_gate_weights(w_lru_gates[0], b_lru_gates[0])
    wab = w_attn_branch[0].astype(BF16)
    wlb = w_lru_branch[0].astype(BF16)
    wo = w_out[0].astype(BF16)
    wfi = w_ffn_in[0].astype(BF16)
    wfo = w_ffn_out[0].astype(BF16)
    ln = jnp.concatenate([ln1_g, ln1_b, ln2_g, ln2_b], axis=0).astype(F32)

    nb = c_prompt.shape[0]
    mod_all = _ada(jnp.concatenate([c_prompt, c_sample], axis=0), w_ada[0], b_ada[0])
    mod_all = mod_all.reshape(mod_all.shape[0], 6, D_MODEL)

    def trunk(x, mod):
        cos_t, sin_t = _rope_tables(x.shape[1])
        q, k, v, xr, yr, ga, gl = _inproj(x, mod, cos_t, sin_t, w_in_b)
        attn = _attention(q, k, v, lam_vecs, subln_g[0].astype(F32), lambda_init)
        rec = _lru(xr, yr, conv_w[0], conv_b[0], wg, lru_lambda[0])
        return _out(x, attn, rec, ga, gl, mod, wab, wlb, wo, wfi, wfo, ln)

    return (trunk(x_prompt, mod_all[:nb]), trunk(x_sample, mod_all[nb:]))
```

```python
import functools
import math

import jax
import jax.numpy as jnp
from jax import lax
from jax.experimental import pallas as pl
from jax.experimental.pallas import tpu as pltpu

D_MODEL = 1024
N_HEADS = 8
HEAD_DIM = 64
V_DIM = 2 * HEAD_DIM
ROPE_THETA = 10000.0
RMS_EPS = 1e-5
LRU_WIDTH = 1024
LRU_BLOCKS = 16
LRU_BLOCK = LRU_WIDTH // LRU_BLOCKS
LRU_C = 8.0
CONV_WIDTH = 4
CONV_LEFT = 2
D_FF = 2816
DEPTH = 1
ALPHA = (2.0 * DEPTH) ** 0.25
LN_EPS = 1e-5
LOG2E = 1.4426950408889634

LANES = 128
SUBLANES = 8
VMEM_LIMIT = 56 * 1024 * 1024

F32 = jnp.float32
BF16 = jnp.bfloat16


def _cparams(n_axes):
    return pltpu.CompilerParams(
        dimension_semantics=("arbitrary",) * n_axes,
        vmem_limit_bytes=VMEM_LIMIT,
    )


def _resident(shape):
    nd = len(shape)
    return pl.BlockSpec(shape, lambda *_: (0,) * nd, pipeline_mode=pl.Buffered(1))


def _layer_norm(x):
    mu = jnp.mean(x, axis=-1, keepdims=True)
    xc = x - mu
    var = jnp.mean(xc * xc, axis=-1, keepdims=True)
    return xc * lax.rsqrt(var + LN_EPS)


def _sigmoid(x):
    return 0.5 * jnp.tanh(0.5 * x) + 0.5


def _split_bf16(x):
    hi = x.astype(BF16)
    lo = (x - hi.astype(F32)).astype(BF16)
    return hi, lo


def _ada_kernel(c_ref, w_ref, b_ref, o_ref):
    c = c_ref[...]
    s_hi, s_lo = _split_bf16(c * _sigmoid(c))
    w_hi, w_lo = _split_bf16(w_ref[...])
    acc = jnp.dot(s_hi, w_hi, preferred_element_type=F32)
    acc += jnp.dot(s_lo, w_hi, preferred_element_type=F32)
    acc += jnp.dot(s_hi, w_lo, preferred_element_type=F32)
    o_ref[...] = acc + b_ref[...]


def _ada(c, w_ada, b_ada):
    rows = c.shape[0]
    n_out = w_ada.shape[1]
    tn = 1024
    return pl.pallas_call(
        _ada_kernel,
        grid=(n_out // tn,),
        in_specs=[
            pl.BlockSpec((rows, D_MODEL), lambda j: (0, 0)),
            pl.BlockSpec((D_MODEL, tn), lambda j: (0, j)),
            pl.BlockSpec((1, tn), lambda j: (0, j)),
        ],
        out_specs=pl.BlockSpec((rows, tn), lambda j: (0, j)),
        out_shape=jax.ShapeDtypeStruct((rows, n_out), F32),
        compiler_params=_cparams(1),
        name="ada",
    )(c, w_ada, b_ada.reshape(1, n_out))


N_PROJ = 7
IN_ROWS = 256


def _inproj_kernel(x_ref, mod_ref, cos_ref, sin_ref, w_ref, *out_refs):
    tm = x_ref.shape[0]
    groups = [slice(r * IN_ROWS, (r + 1) * IN_ROWS) for r in range(tm // IN_ROWS)]
    hbs = []
    for rows in groups:
        h = _layer_norm(x_ref[rows, :]) * (1.0 + mod_ref[1:2, :]) + mod_ref[0:1, :]
        hbs.append(h.astype(BF16))
    q_scale = HEAD_DIM ** -0.5 * LOG2E
    for g, o_ref in enumerate(out_refs):
        for rows, hb in zip(groups, hbs):
            p = jnp.dot(hb, w_ref[:, g * D_MODEL:(g + 1) * D_MODEL], preferred_element_type=F32)
            if g < 2:
                cos = cos_ref[rows, :]
                sin = sin_ref[rows, :]
                for hh in range(N_HEADS):
                    t = p[:, hh * LANES:(hh + 1) * LANES]
                    r = t * cos + pltpu.roll(t, LANES // 2, axis=1) * sin
                    if g == 0:
                        r = r * q_scale
                    o_ref[rows, hh * LANES:(hh + 1) * LANES] = r.astype(o_ref.dtype)
            else:
                o_ref[rows, :] = p.astype(o_ref.dtype)


def _inproj(x, mod, cos_t, sin_t, w_in_b):
    B, S, _ = x.shape
    tm = min(2 * IN_ROWS, S)
    assert S % tm == 0 and tm % IN_ROWS == 0
    tok = lambda: pl.BlockSpec((None, tm, D_MODEL), lambda b, i: (b, i, 0))
    return pl.pallas_call(
        _inproj_kernel,
        grid=(B, S // tm),
        in_specs=[
            tok(),
            pl.BlockSpec((None, 6, D_MODEL), lambda b, i: (b, 0, 0)),
            pl.BlockSpec((tm, LANES), lambda b, i: (i, 0)),
            pl.BlockSpec((tm, LANES), lambda b, i: (i, 0)),
            _resident(w_in_b.shape),
        ],
        out_specs=[tok() for _ in range(N_PROJ)],
        out_shape=[jax.ShapeDtypeStruct((B, S, D_MODEL), BF16) for _ in range(N_PROJ)],
        compiler_params=_cparams(2),
        name="inproj",
    )(x, mod, cos_t, sin_t, w_in_b)


ONES_ROWS = 16


def _attn_kernel(lam_ref, g_ref, q_ref, k_ref, v_ref, o_ref, vt_ref, s_ref, acc_ref, *, tq, kc, lambda_init):
    S = k_ref.shape[0]
    nq = S // tq
    nc = S // kc

    vt_ref[0:V_DIM, :] = v_ref[...].astype(F32).T.astype(BF16)
    vt_ref[V_DIM:V_DIM + ONES_ROWS, :] = jnp.ones((ONES_ROWS, S), BF16)

    lv = lam_ref[...]
    lam = (jnp.exp(jnp.sum(lv[0:1] * lv[1:2], axis=1, keepdims=True))
           - jnp.exp(jnp.sum(lv[2:3] * lv[3:4], axis=1, keepdims=True)) + lambda_init)

    lane = lax.broadcasted_iota(jnp.int32, (tq, LANES), 1)
    is_map0 = (lane % (2 * 32)) < 32

    def masked_queries(i):
        q = q_ref[pl.ds(pl.multiple_of(i * tq, tq), tq), :]
        zero = jnp.zeros_like(q)
        return jnp.concatenate([jnp.where(is_map0, q, zero), jnp.where(is_map0, zero, q)], axis=0)

    def score_chunk(qm, slot, c, m):
        s = lax.dot_general(k_ref[c * kc:(c + 1) * kc, :], qm, (((1,), (1,)), ((), ())),
                            preferred_element_type=F32)
        s_ref[slot, c * kc:(c + 1) * kc, :] = s
        m_c = jnp.max(s, axis=0, keepdims=True)
        return m_c if m is None else jnp.maximum(m, m_c)

    def value_chunk(slot, c, m):
        e = jnp.exp2(s_ref[slot, c * kc:(c + 1) * kc, :] - m).astype(BF16)
        part = jnp.dot(vt_ref[:, c * kc:(c + 1) * kc], e, preferred_element_type=F32)
        if c == 0:
            acc_ref[...] = part
        else:
            acc_ref[...] += part

    def finish(i):
        acc = acc_ref[...]
        ot = acc[0:V_DIM] * (1.0 / acc[V_DIM:V_DIM + 1])
        o = ot[:, :tq] - lam * ot[:, tq:]
        o = o * lax.rsqrt(jnp.mean(o * o, axis=0, keepdims=True) + RMS_EPS)
        o = o * g_ref[...] * (1.0 - lambda_init)
        o_ref[pl.ds(pl.multiple_of(i * tq, tq), tq), :] = o.T.astype(o_ref.dtype)

    def scores_only(i, slot):
        qm = masked_queries(i)
        m = None
        for c in range(nc):
            m = score_chunk(qm, slot, c, m)
        return m

    def overlapped(i, slot, m):
        qm_next = masked_queries(i + 1)
        m_next = None
        for c in range(nc):
            m_next = score_chunk(qm_next, 1 - slot, c, m_next)
            value_chunk(slot, c, m)
        finish(i)
        return m_next

    def values_only(i, slot, m):
        for c in range(nc):
            value_chunk(slot, c, m)
        finish(i)

    m_even = scores_only(0, 0)
    if nq == 1:
        values_only(0, 0, m_even)
    else:
        def pair(j, m):
            m_odd = overlapped(2 * j, 0, m)
            return overlapped(2 * j + 1, 1, m_odd)

        m_even = lax.fori_loop(0, nq // 2 - 1, pair, m_even)
        m_odd = overlapped(nq - 2, 0, m_even)
        values_only(nq - 1, 1, m_odd)


def _attention(q, k, v, lam_vecs, subln_g, lambda_init):
    B, S, _ = q.shape
    tq = min(256, S)
    kc = min(512, S)
    nq = S // tq
    assert S % tq == 0 and S % kc == 0 and (nq == 1 or nq % 2 == 0)
    col = lambda: pl.BlockSpec((None, S, LANES), lambda b, h: (b, 0, h))
    return pl.pallas_call(
        functools.partial(_attn_kernel, tq=tq, kc=kc, lambda_init=lambda_init),
        grid=(B, N_HEADS),
        in_specs=[
            pl.BlockSpec((4, HEAD_DIM), lambda b, h: (0, 0)),
            pl.BlockSpec((V_DIM, 1), lambda b, h: (0, 0)),
            col(), col(), col(),
        ],
        out_specs=col(),
        out_shape=jax.ShapeDtypeStruct((B, S, D_MODEL), BF16),
        scratch_shapes=[
            pltpu.VMEM((V_DIM + ONES_ROWS, S), BF16),
            pltpu.VMEM((2, S, 2 * tq), F32),
            pltpu.VMEM((V_DIM + ONES_ROWS, 2 * tq), F32),
        ],
        compiler_params=_cparams(2),
        name="attn",
    )(lam_vecs, subln_g.reshape(V_DIM, 1), q, k, v)


LRU_TS = 256
LRU_LB = 2
SCAN_GROUP = 8


def _lru_kernel(xr_ref, yr_ref, cw_ref, cb_ref, wg_ref, lam_ref, o_ref,
                x_s, af_s, df_s, ab_s, db_s, rf_s, rb_s, *, ts):
    S = xr_ref.shape[0]
    chunk = S // SUBLANES
    n_ts = S // ts
    blocks = range(LRU_LB)

    def lanes(lb):
        return slice(lb * LANES, (lb + 1) * LANES)

    def interleaved(i):
        first = i * ts
        return pl.ds((first % chunk) * SUBLANES + first // chunk, ts, stride=SUBLANES)

    for lb in blocks:
        x_s[lb, 0:SUBLANES, :] = jnp.zeros((SUBLANES, LANES), F32)
        x_s[lb, S + SUBLANES:S + 2 * SUBLANES, :] = jnp.zeros((SUBLANES, LANES), F32)

    def stage(i, _):
        off = pl.multiple_of(i * ts, ts)
        for lb in blocks:
            x_s[lb, pl.ds(off + SUBLANES, ts), :] = xr_ref[pl.ds(off, ts), lanes(lb)].astype(F32)
        return 0

    lax.fori_loop(0, n_ts, stage, 0)

    cw = cw_ref[...]
    cb = cb_ref[...]
    z = -lam_ref[...]
    softplus = jnp.maximum(z, 0.0) + jnp.log1p(jnp.exp(-jnp.abs(z)))
    half_neg_c_sp = (-0.5 * LRU_C) * softplus
    lane = lax.broadcasted_iota(jnp.int32, (ts, LANES), 1)
    bias_lanes = jnp.where(lane < 2, 1.0, 0.0).astype(BF16)
    tiny = float(jnp.finfo(F32).tiny)

    def gates(i, _):
        off = pl.multiple_of(i * ts, ts)
        for lb in blocks:
            xc = cb[:, lanes(lb)]
            for tap in range(CONV_WIDTH):
                xc = xc + cw[tap:tap + 1, lanes(lb)] * x_s[lb, pl.ds(off + SUBLANES + tap - CONV_LEFT, ts), :]
            lhs = jnp.concatenate([xc.astype(BF16), bias_lanes], axis=1)
            g = jnp.dot(lhs, wg_ref[lb], preferred_element_type=F32)
            sx = -math.sqrt(0.5) * xc
            for d, (a_s, d_s) in enumerate(((af_s, df_s), (ab_s, db_s))):
                k = half_neg_c_sp[d:d + 1, lanes(lb)]
                log_a = k * jnp.tanh(g[:, (2 * d) * LANES:(2 * d + 1) * LANES]) + k
                a_s[lb, interleaved(i), :] = jnp.exp(log_a)
                gated_x = sx * jnp.tanh(g[:, (2 * d + 1) * LANES:(2 * d + 2) * LANES]) + sx
                th = jnp.tanh(log_a)
                root = th * lax.rsqrt(jnp.maximum(th * (th - 1.0), tiny))
                d_s[lb, interleaved(i), :] = root * gated_x
        return 0

    lax.fori_loop(0, n_ts, gates, 0, unroll=math.gcd(n_ts, 2))

    zeros = jnp.zeros((SUBLANES, LANES), F32)
    ones = jnp.ones((SUBLANES, LANES), F32)

    n_groups = chunk // SCAN_GROUP
    group_rows = SCAN_GROUP * SUBLANES

    def group(g):
        return pl.ds(pl.multiple_of(g * group_rows, group_rows), group_rows)

    def scan_group(a, d, order, h, p):
        states = {}
        for k0, k1 in zip(order[0::2], order[1::2]):
            a0 = a[k0 * SUBLANES:(k0 + 1) * SUBLANES]
            d0 = d[k0 * SUBLANES:(k0 + 1) * SUBLANES]
            a1 = a[k1 * SUBLANES:(k1 + 1) * SUBLANES]
            a2 = a1 * a0
            d2 = a1 * d0 + d[k1 * SUBLANES:(k1 + 1) * SUBLANES]
            states[k0] = a0 * h + d0
            h = a2 * h + d2
            states[k1] = h
            p = a2 * p
        return h, p, states

    forward = list(range(SCAN_GROUP))
    backward = forward[::-1]

    def local_scan(g, carry):
        gb = n_groups - 1 - g
        out = []
        for lb, (hf, pf, hb, pb) in zip(blocks, carry):
            hf, pf, _ = scan_group(af_s[lb, group(g), :], df_s[lb, group(g), :], forward, hf, pf)
            hb, pb, _ = scan_group(ab_s[lb, group(gb), :], db_s[lb, group(gb), :], backward, hb, pb)
            out.append((hf, pf, hb, pb))
        return tuple(out)

    ends = lax.fori_loop(0, n_groups, local_scan, tuple((zeros, ones, zeros, ones) for _ in blocks))

    row = lax.broadcasted_iota(jnp.int32, (SUBLANES, LANES), 0)

    def pick(v, s):
        return jnp.sum(jnp.where(row == s, v, 0.0), axis=0, keepdims=True)

    entering = []
    for hf_last, pf_last, hb_first, pb_first in ends:
        cf = zeros
        for s in range(1, SUBLANES):
            nxt = pick(hf_last, s - 1) + pick(pf_last, s - 1) * pick(cf, s - 1)
            cf = jnp.where(row == s, nxt, cf)
        cbk = zeros
        for s in range(SUBLANES - 2, -1, -1):
            nxt = pick(hb_first, s + 1) + pick(pb_first, s + 1) * pick(cbk, s + 1)
            cbk = jnp.where(row == s, nxt, cbk)
        entering.append((cf, cbk))

    def full_scan(g, carry):
        gb = n_groups - 1 - g
        out = []
        for lb, (hf, hb) in zip(blocks, carry):
            hf, _, states = scan_group(af_s[lb, group(g), :], df_s[lb, group(g), :], forward, hf, ones)
            rf_s[lb, group(g), :] = jnp.concatenate([states[k] for k in forward], axis=0)
            hb, _, states = scan_group(ab_s[lb, group(gb), :], db_s[lb, group(gb), :], backward, hb, ones)
            rb_s[lb, group(gb), :] = jnp.concatenate([states[k] for k in forward], axis=0)
            out.append((hf, hb))
        return tuple(out)

    lax.fori_loop(0, n_groups, full_scan, tuple(entering))

    def emit(i, _):
        off = pl.multiple_of(i * ts, ts)
        for lb in blocks:
            y = yr_ref[pl.ds(off, ts), lanes(lb)].astype(F32)
            rec = rf_s[lb, interleaved(i), :] + rb_s[lb, interleaved(i), :]
            o_ref[pl.ds(off, ts), lanes(lb)] = (rec * jax.nn.gelu(y)).astype(o_ref.dtype)
        return 0

    lax.fori_loop(0, n_ts, emit, 0)


def _lru(xr, yr, conv_w, conv_b, wg, lru_lambda):
    B, S, _ = xr.shape
    width = LRU_LB * LANES
    chunk = S // SUBLANES
    ts = min(LRU_TS, chunk)
    assert S % SUBLANES == 0 and chunk % ts == 0 and chunk % SCAN_GROUP == 0 and LRU_WIDTH % width == 0
    col = lambda: pl.BlockSpec((None, S, width), lambda b, c: (b, 0, c))
    seq = lambda: pltpu.VMEM((LRU_LB, S, LANES), F32)
    return pl.pallas_call(
        functools.partial(_lru_kernel, ts=ts),
        grid=(B, LRU_WIDTH // width),
        in_specs=[
            col(),
            col(),
            pl.BlockSpec((CONV_WIDTH, width), lambda b, c: (0, c)),
            pl.BlockSpec((1, width), lambda b, c: (0, c)),
            pl.BlockSpec((LRU_LB, 2 * LANES, 4 * LANES), lambda b, c: (c, 0, 0)),
            pl.BlockSpec((2, width), lambda b, c: (0, c)),
        ],
        out_specs=col(),
        out_shape=jax.ShapeDtypeStruct((B, S, LRU_WIDTH), BF16),
        scratch_shapes=[
            pltpu.VMEM((LRU_LB, S + 2 * SUBLANES, LANES), F32),
            seq(), seq(), seq(), seq(), seq(), seq(),
        ],
        compiler_params=_cparams(2),
        name="lru",
    )(xr, yr, conv_w, conv_b.reshape(1, LRU_WIDTH), wg, lru_lambda)


def _pack_gate_weights(w_gates, b_gates):
    ncb = LRU_WIDTH // LANES
    per = LANES // LRU_BLOCK
    w = w_gates.reshape(4, ncb, per, LRU_BLOCK, LRU_BLOCK)
    eye = jnp.eye(per, dtype=w.dtype)
    dense = jnp.einsum("gcpde,pq->cpdgqe", w, eye)
    dense = (0.5 * dense.reshape(ncb, LANES, 4 * LANES)).astype(BF16)
    bias = 0.5 * b_gates.reshape(4, ncb, LANES).transpose(1, 0, 2).reshape(ncb, 1, 4 * LANES)
    bias_hi, bias_lo = _split_bf16(bias)
    pad = jnp.zeros((ncb, LANES - 2, 4 * LANES), BF16)
    return jnp.concatenate([dense, bias_hi, bias_lo, pad], axis=1)


OUT_ROWS = 256


def _out_kernel(x_ref, attn_ref, rec_ref, ga_ref, gl_ref, mod_ref, wab_ref, wlb_ref, wo_ref,
                wfi_ref, wfo_ref, ln_ref, y_ref):
    tm = x_ref.shape[0]
    groups = [slice(g * OUT_ROWS, (g + 1) * OUT_ROWS) for g in range(tm // OUT_ROWS)]

    def branches(rows):
        a = jnp.dot(attn_ref[rows, :], wab_ref[...], preferred_element_type=F32)
        r = jnp.dot(rec_ref[rows, :], wlb_ref[...], preferred_element_type=F32)
        merged = _sigmoid(ga_ref[rows, :].astype(F32)) * a + _sigmoid(gl_ref[rows, :].astype(F32)) * r
        return merged.astype(BF16)

    def mixer_norm(rows, merged):
        mix = jnp.dot(merged, wo_ref[...], preferred_element_type=F32)
        x1 = _layer_norm(ALPHA * x_ref[rows, :] + mod_ref[2:3, :] * mix) * ln_ref[0:1, :] + ln_ref[1:2, :]
        h = _layer_norm(x1) * (1.0 + mod_ref[4:5, :]) + mod_ref[3:4, :]
        return x1, h.astype(BF16)

    def ffn_hidden(h):
        gu = jnp.dot(h, wfi_ref[...], preferred_element_type=F32)
        gate = gu[:, :D_FF]
        return (gate * _sigmoid(gate) * gu[:, D_FF:]).astype(BF16)

    def ffn_norm(rows, x1, act):
        f = jnp.dot(act, wfo_ref[...], preferred_element_type=F32)
        y_ref[rows, :] = _layer_norm(ALPHA * x1 + mod_ref[5:6, :] * f) * ln_ref[2:3, :] + ln_ref[3:4, :]

    merged = [branches(rows) for rows in groups]
    normed = [mixer_norm(rows, m) for rows, m in zip(groups, merged)]
    acts = [ffn_hidden(h) for _, h in normed]
    for rows, (x1, _), act in zip(groups, normed, acts):
        ffn_norm(rows, x1, act)


def _out(x, attn, rec, ga, gl, mod, wab, wlb, wo, wfi, wfo, ln):
    B, S, _ = x.shape
    tm = min(2 * OUT_ROWS, S)
    assert S % tm == 0 and tm % OUT_ROWS == 0
    tok = lambda: pl.BlockSpec((None, tm, D_MODEL), lambda b, i: (b, i, 0))
    return pl.pallas_call(
        _out_kernel,
        grid=(B, S // tm),
        in_specs=[
            tok(), tok(), tok(), tok(), tok(),
            pl.BlockSpec((None, 6, D_MODEL), lambda b, i: (b, 0, 0)),
            _resident(wab.shape), _resident(wlb.shape), _resident(wo.shape),
            _resident(wfi.shape), _resident(wfo.shape),
            _resident(ln.shape),
        ],
        out_specs=tok(),
        out_shape=jax.ShapeDtypeStruct((B, S, D_MODEL), F32),
        compiler_params=_cparams(2),
        name="out",
    )(x, attn, rec, ga, gl, mod, wab, wlb, wo, wfi, wfo, ln)


def _rope_tables(seq):
    inv = 1.0 / (ROPE_THETA ** (jnp.arange(0, HEAD_DIM, 2, dtype=F32) / HEAD_DIM))
    ang = jnp.arange(seq, dtype=F32)[:, None] * inv[None, :]
    cos, sin = jnp.cos(ang), jnp.sin(ang)
    return (jnp.concatenate([cos, cos, cos, cos], axis=1),
            jnp.concatenate([-sin, -sin, sin, sin], axis=1))


def _permute_qk_columns(w):
    d = w.shape[0]
    return w.reshape(d, N_HEADS, 2, 2, HEAD_DIM // 2).transpose(0, 1, 3, 2, 4).reshape(d, N_HEADS * LANES)


def kernel(x_prompt, x_sample, c_prompt, c_sample, w_ada, b_ada, w_in, lambda_q1, lambda_k1, lambda_q2, lambda_k2, subln_g, conv_w, conv_b, w_lru_gates, b_lru_gates, lru_lambda, w_attn_branch, w_lru_branch, w_out, ln1_g, ln1_b, w_ffn_in, w_ffn_out, ln2_g, ln2_b):
    assert w_ada.shape[0] == DEPTH == 1
    lambda_init = 0.8 - 0.6 * math.exp(-0.3 * 0)

    w_in0 = w_in[0]
    w_in_b = jnp.concatenate(
        [_permute_qk_columns(w_in0[:, :D_MODEL]), _permute_qk_columns(w_in0[:, D_MODEL:2 * D_MODEL]),
         w_in0[:, 2 * D_MODEL:]], axis=1).astype(BF16)
    lam_vecs = jnp.concatenate([lambda_q1, lambda_k1, lambda_q2, lambda_k2], axis=0).astype(F32)
    wg = _pack_gate_weights(w_lru_gates[0], b_lru_gates[0])
    wab = w_attn_branch[0].astype(BF16)
    wlb = w_lru_branch[0].astype(BF16)
    wo = w_out[0].astype(BF16)
    wfi = w_ffn_in[0].astype(BF16)
    wfo = w_ffn_out[0].astype(BF16)
    ln = jnp.concatenate([ln1_g, ln1_b, ln2_g, ln2_b], axis=0).astype(F32)

    nb = c_prompt.shape[0]
    mod_all = _ada(jnp.concatenate([c_prompt, c_sample], axis=0), w_ada[0], b_ada[0])
    mod_all = mod_all.reshape(mod_all.shape[0], 6, D_MODEL)

    def trunk(x, mod):
        cos_t, sin_t = _rope_tables(x.shape[1])
        q, k, v, xr, yr, ga, gl = _inproj(x, mod, cos_t, sin_t, w_in_b)
        attn = _attention(q, k, v, lam_vecs, subln_g[0].astype(F32), lambda_init)
        rec = _lru(xr, yr, conv_w[0], conv_b[0], wg, lru_lambda[0])
        return _out(x, attn, rec, ga, gl, mod, wab, wlb, wo, wfi, wfo, ln)

    return (trunk(x_prompt, mod_all[:nb]), trunk(x_sample, mod_all[nb:]))
```

```python
import functools
import math

import jax
import jax.numpy as jnp
from jax import lax
from jax.experimental import pallas as pl
from jax.experimental.pallas import tpu as pltpu

D_MODEL = 1024
N_HEADS = 8
HEAD_DIM = 64
V_DIM = 2 * HEAD_DIM
ROPE_THETA = 10000.0
RMS_EPS = 1e-5
LRU_WIDTH = 1024
LRU_BLOCKS = 16
LRU_BLOCK = LRU_WIDTH // LRU_BLOCKS
LRU_C = 8.0
CONV_WIDTH = 4
CONV_LEFT = 2
D_FF = 2816
DEPTH = 1
ALPHA = (2.0 * DEPTH) ** 0.25
LN_EPS = 1e-5
LOG2E = 1.4426950408889634

LANES = 128
SUBLANES = 8
VMEM_LIMIT = 56 * 1024 * 1024

F32 = jnp.float32
BF16 = jnp.bfloat16


def _cparams(n_axes):
    return pltpu.CompilerParams(
        dimension_semantics=("arbitrary",) * n_axes,
        vmem_limit_bytes=VMEM_LIMIT,
    )


def _resident(shape):
    nd = len(shape)
    return pl.BlockSpec(shape, lambda *_: (0,) * nd, pipeline_mode=pl.Buffered(1))


def _layer_norm(x):
    mu = jnp.mean(x, axis=-1, keepdims=True)
    xc = x - mu
    var = jnp.mean(xc * xc, axis=-1, keepdims=True)
    return xc * lax.rsqrt(var + LN_EPS)


def _sigmoid(x):
    return 0.5 * jnp.tanh(0.5 * x) + 0.5


def _split_bf16(x):
    hi = x.astype(BF16)
    lo = (x - hi.astype(F32)).astype(BF16)
    return hi, lo


def _ada_kernel(c_ref, w_ref, b_ref, o_ref):
    c = c_ref[...]
    s_hi, s_lo = _split_bf16(c * _sigmoid(c))
    w_hi, w_lo = _split_bf16(w_ref[...])
    acc = jnp.dot(s_hi, w_hi, preferred_element_type=F32)
    acc += jnp.dot(s_lo, w_hi, preferred_element_type=F32)
    acc += jnp.dot(s_hi, w_lo, preferred_element_type=F32)
    o_ref[...] = acc + b_ref[...]


def _ada(c, w_ada, b_ada):
    rows = c.shape[0]
    n_out = w_ada.shape[1]
    tn = 1024
    return pl.pallas_call(
        _ada_kernel,
        grid=(n_out // tn,),
        in_specs=[
            pl.BlockSpec((rows, D_MODEL), lambda j: (0, 0)),
            pl.BlockSpec((D_MODEL, tn), lambda j: (0, j)),
            pl.BlockSpec((1, tn), lambda j: (0, j)),
        ],
        out_specs=pl.BlockSpec((rows, tn), lambda j: (0, j)),
        out_shape=jax.ShapeDtypeStruct((rows, n_out), F32),
        compiler_params=_cparams(1),
        name="ada",
    )(c, w_ada, b_ada.reshape(1, n_out))


N_PROJ = 7
IN_ROWS = 256


def _inproj_kernel(x_ref, mod_ref, cos_ref, sin_ref, w_ref, *out_refs):
    tm = x_ref.shape[0]
    groups = [slice(r * IN_ROWS, (r + 1) * IN_ROWS) for r in range(tm // IN_ROWS)]
    hbs = []
    for rows in groups:
        h = _layer_norm(x_ref[rows, :]) * (1.0 + mod_ref[1:2, :]) + mod_ref[0:1, :]
        hbs.append(h.astype(BF16))
    q_scale = HEAD_DIM ** -0.5 * LOG2E
    for g, o_ref in enumerate(out_refs):
        for rows, hb in zip(groups, hbs):
            p = jnp.dot(hb, w_ref[:, g * D_MODEL:(g + 1) * D_MODEL], preferred_element_type=F32)
            if g < 2:
                cos = cos_ref[rows, :]
                sin = sin_ref[rows, :]
                for hh in range(N_HEADS):
                    t = p[:, hh * LANES:(hh + 1) * LANES]
                    r = t * cos + pltpu.roll(t, LANES // 2, axis=1) * sin
                    if g == 0:
                        r = r * q_scale
                    o_ref[rows, hh * LANES:(hh + 1) * LANES] = r.astype(o_ref.dtype)
            else:
                o_ref[rows, :] = p.astype(o_ref.dtype)


def _inproj(x, mod, cos_t, sin_t, w_in_b):
    B, S, _ = x.shape
    tm = min(2 * IN_ROWS, S)
    assert S % tm == 0 and tm % IN_ROWS == 0
    tok = lambda: pl.BlockSpec((None, tm, D_MODEL), lambda b, i: (b, i, 0))
    return pl.pallas_call(
        _inproj_kernel,
        grid=(B, S // tm),
        in_specs=[
            tok(),
            pl.BlockSpec((None, 6, D_MODEL), lambda b, i: (b, 0, 0)),
            pl.BlockSpec((tm, LANES), lambda b, i: (i, 0)),
            pl.BlockSpec((tm, LANES), lambda b, i: (i, 0)),
            _resident(w_in_b.shape),
        ],
        out_specs=[tok() for _ in range(N_PROJ)],
        out_shape=[jax.ShapeDtypeStruct((B, S, D_MODEL), BF16) for _ in range(N_PROJ)],
        compiler_params=_cparams(2),
        name="inproj",
    )(x, mod, cos_t, sin_t, w_in_b)


ONES_ROWS = 16


def _attn_kernel(lam_ref, g_ref, q_ref, k_ref, v_ref, o_ref, vt_ref, s_ref, acc_ref, *, tq, kc, lambda_init):
    S = k_ref.shape[0]
    nq = S // tq
    nc = S // kc

    vt_ref[0:V_DIM, :] = v_ref[...].astype(F32).T.astype(BF16)
    vt_ref[V_DIM:V_DIM + ONES_ROWS, :] = jnp.ones((ONES_ROWS, S), BF16)

    lv = lam_ref[...]
    lam = (jnp.exp(jnp.sum(lv[0:1] * lv[1:2], axis=1, keepdims=True))
           - jnp.exp(jnp.sum(lv[2:3] * lv[3:4], axis=1, keepdims=True)) + lambda_init)

    lane = lax.broadcasted_iota(jnp.int32, (tq, LANES), 1)
    is_map0 = (lane % (2 * 32)) < 32

    def masked_queries(i):
        q = q_ref[pl.ds(pl.multiple_of(i * tq, tq), tq), :]
        zero = jnp.zeros_like(q)
        return jnp.concatenate([jnp.where(is_map0, q, zero), jnp.where(is_map0, zero, q)], axis=0)

    def score_chunk(qm, slot, c, m):
        s = lax.dot_general(k_ref[c * kc:(c + 1) * kc, :], qm, (((1,), (1,)), ((), ())),
                            preferred_element_type=F32)
        s_ref[slot, c * kc:(c + 1) * kc, :] = s
        m_c = jnp.max(s, axis=0, keepdims=True)
        return m_c if m is None else jnp.maximum(m, m_c)

    def value_chunk(slot, c, m):
        e = jnp.exp2(s_ref[slot, c * kc:(c + 1) * kc, :] - m).astype(BF16)
        part = jnp.dot(vt_ref[:, c * kc:(c + 1) * kc], e, preferred_element_type=F32)
        if c == 0:
            acc_ref[...] = part
        else:
            acc_ref[...] += part

    def finish(i):
        acc = acc_ref[...]
        ot = acc[0:V_DIM] * (1.0 / acc[V_DIM:V_DIM + 1])
        o = ot[:, :tq] - lam * ot[:, tq:]
        o = o * lax.rsqrt(jnp.mean(o * o, axis=0, keepdims=True) + RMS_EPS)
        o = o * g_ref[...] * (1.0 - lambda_init)
        o_ref[pl.ds(pl.multiple_of(i * tq, tq), tq), :] = o.T.astype(o_ref.dtype)

    def scores_only(i, slot):
        qm = masked_queries(i)
        m = None
        for c in range(nc):
            m = score_chunk(qm, slot, c, m)
        return m

    def overlapped(i, slot, m):
        qm_next = masked_queries(i + 1)
        m_next = None
        for c in range(nc):
            m_next = score_chunk(qm_next, 1 - slot, c, m_next)
            value_chunk(slot, c, m)
        finish(i)
        return m_next

    def values_only(i, slot, m):
        for c in range(nc):
            value_chunk(slot, c, m)
        finish(i)

    m_even = scores_only(0, 0)
    if nq == 1:
        values_only(0, 0, m_even)
    else:
        def pair(j, m):
            m_odd = overlapped(2 * j, 0, m)
            return overlapped(2 * j + 1, 1, m_odd)

        m_even = lax.fori_loop(0, nq // 2 - 1, pair, m_even)
        m_odd = overlapped(nq - 2, 0, m_even)
        values_only(nq - 1, 1, m_odd)


def _attention(q, k, v, lam_vecs, subln_g, lambda_init):
    B, S, _ = q.shape
    tq = min(256, S)
    kc = min(512, S)
    nq = S // tq
    assert S % tq == 0 and S % kc == 0 and (nq == 1 or nq % 2 == 0)
    col = lambda: pl.BlockSpec((None, S, LANES), lambda b, h: (b, 0, h))
    return pl.pallas_call(
        functools.partial(_attn_kernel, tq=tq, kc=kc, lambda_init=lambda_init),
        grid=(B, N_HEADS),
        in_specs=[
            pl.BlockSpec((4, HEAD_DIM), lambda b, h: (0, 0)),
            pl.BlockSpec((V_DIM, 1), lambda b, h: (0, 0)),
            col(), col(), col(),
        ],
        out_specs=col(),
        out_shape=jax.ShapeDtypeStruct((B, S, D_MODEL), BF16),
        scratch_shapes=[
            pltpu.VMEM((V_DIM + ONES_ROWS, S), BF16),
            pltpu.VMEM((2, S, 2 * tq), F32),
            pltpu.VMEM((V_DIM + ONES_ROWS, 2 * tq), F32),
        ],
        compiler_params=_cparams(2),
        name="attn",
    )(lam_vecs, subln_g.reshape(V_DIM, 1), q, k, v)


LRU_TS = 256
LRU_LB = 2
SCAN_GROUP = 16


def _lru_kernel(xr_ref, yr_ref, cw_ref, cb_ref, wg_ref, lam_ref, o_ref,
                x_s, af_s, df_s, ab_s, db_s, rf_s, rb_s, *, ts):
    S = xr_ref.shape[0]
    chunk = S // SUBLANES
    n_ts = S // ts
    blocks = range(LRU_LB)

    def lanes(lb):
        return slice(lb * LANES, (lb + 1) * LANES)

    def interleaved(i):
        first = i * ts
        return pl.ds((first % chunk) * SUBLANES + first // chunk, ts, stride=SUBLANES)

    for lb in blocks:
        x_s[lb, 0:SUBLANES, :] = jnp.zeros((SUBLANES, LANES), F32)
        x_s[lb, S + SUBLANES:S + 2 * SUBLANES, :] = jnp.zeros((SUBLANES, LANES), F32)

    def stage(i, _):
        off = pl.multiple_of(i * ts, ts)
        for lb in blocks:
            x_s[lb, pl.ds(off + SUBLANES, ts), :] = xr_ref[pl.ds(off, ts), lanes(lb)].astype(F32)
        return 0

    lax.fori_loop(0, n_ts, stage, 0, unroll=math.gcd(n_ts, 4))

    cw = cw_ref[...]
    cb = cb_ref[...]
    z = -lam_ref[...]
    softplus = jnp.maximum(z, 0.0) + jnp.log1p(jnp.exp(-jnp.abs(z)))
    half_neg_c_sp = (-0.5 * LRU_C) * softplus
    lane = lax.broadcasted_iota(jnp.int32, (ts, LANES), 1)
    bias_lanes = jnp.where(lane < 2, 1.0, 0.0).astype(BF16)
    tiny = float(jnp.finfo(F32).tiny)

    def gates(i, _):
        off = pl.multiple_of(i * ts, ts)
        for lb in blocks:
            xc = cb[:, lanes(lb)]
            for tap in range(CONV_WIDTH):
                xc = xc + cw[tap:tap + 1, lanes(lb)] * x_s[lb, pl.ds(off + SUBLANES + tap - CONV_LEFT, ts), :]
            lhs = jnp.concatenate([xc.astype(BF16), bias_lanes], axis=1)
            g = jnp.dot(lhs, wg_ref[lb], preferred_element_type=F32)
            sx = -math.sqrt(0.5) * xc
            for d, (a_s, d_s) in enumerate(((af_s, df_s), (ab_s, db_s))):
                k = half_neg_c_sp[d:d + 1, lanes(lb)]
                log_a = k * jnp.tanh(g[:, (2 * d) * LANES:(2 * d + 1) * LANES]) + k
                a_s[lb, interleaved(i), :] = jnp.exp(log_a)
                gated_x = sx * jnp.tanh(g[:, (2 * d + 1) * LANES:(2 * d + 2) * LANES]) + sx
                th = jnp.tanh(log_a)
                root = th * lax.rsqrt(jnp.maximum(th * (th - 1.0), tiny))
                d_s[lb, interleaved(i), :] = root * gated_x
        return 0

    lax.fori_loop(0, n_ts, gates, 0, unroll=math.gcd(n_ts, 8))

    zeros = jnp.zeros((SUBLANES, LANES), F32)
    ones = jnp.ones((SUBLANES, LANES), F32)

    n_groups = chunk // SCAN_GROUP
    group_rows = SCAN_GROUP * SUBLANES

    def group(g):
        return pl.ds(pl.multiple_of(g * group_rows, group_rows), group_rows)

    def scan_group(a, d, order, h, p):
        states = {}
        for k0, k1 in zip(order[0::2], order[1::2]):
            a0 = a[k0 * SUBLANES:(k0 + 1) * SUBLANES]
            d0 = d[k0 * SUBLANES:(k0 + 1) * SUBLANES]
            a1 = a[k1 * SUBLANES:(k1 + 1) * SUBLANES]
            a2 = a1 * a0
            d2 = a1 * d0 + d[k1 * SUBLANES:(k1 + 1) * SUBLANES]
            states[k0] = a0 * h + d0
            h = a2 * h + d2
            states[k1] = h
            p = a2 * p
        return h, p, states

    forward = list(range(SCAN_GROUP))
    backward = forward[::-1]

    def local_scan(g, carry):
        gb = n_groups - 1 - g
        out = []
        for lb, (hf, pf, hb, pb) in zip(blocks, carry):
            hf, pf, _ = scan_group(af_s[lb, group(g), :], df_s[lb, group(g), :], forward, hf, pf)
            hb, pb, _ = scan_group(ab_s[lb, group(gb), :], db_s[lb, group(gb), :], backward, hb, pb)
            out.append((hf, pf, hb, pb))
        return tuple(out)

    ends = lax.fori_loop(0, n_groups, local_scan, tuple((zeros, ones, zeros, ones) for _ in blocks))

    row = lax.broadcasted_iota(jnp.int32, (SUBLANES, LANES), 0)

    def pick(v, s):
        return jnp.sum(jnp.where(row == s, v, 0.0), axis=0, keepdims=True)

    entering = []
    for hf_last, pf_last, hb_first, pb_first in ends:
        cf = zeros
        for s in range(1, SUBLANES):
            nxt = pick(hf_last, s - 1) + pick(pf_last, s - 1) * pick(cf, s - 1)
            cf = jnp.where(row == s, nxt, cf)
        cbk = zeros
        for s in range(SUBLANES - 2, -1, -1):
            nxt = pick(hb_first, s + 1) + pick(pb_first, s + 1) * pick(cbk, s + 1)
            cbk = jnp.where(row == s, nxt, cbk)
        entering.append((cf, cbk))

    def full_scan(g, carry):
        gb = n_groups - 1 - g
        out = []
        for lb, (hf, hb) in zip(blocks, carry):
            hf, _, states = scan_group(af_s[lb, group(g), :], df_s[lb, group(g), :], forward, hf, ones)
            rf_s[lb, group(g), :] = jnp.concatenate([states[k] for k in forward], axis=0)
            hb, _, states = scan_group(ab_s[lb, group(gb), :], db_s[lb, group(gb), :], backward, hb, ones)
            rb_s[lb, group(gb), :] = jnp.concatenate([states[k] for k in forward], axis=0)
            out.append((hf, hb))
        return tuple(out)

    lax.fori_loop(0, n_groups, full_scan, tuple(entering))

    def emit(i, _):
        off = pl.multiple_of(i * ts, ts)
        for lb in blocks:
            y = yr_ref[pl.ds(off, ts), lanes(lb)].astype(F32)
            rec = rf_s[lb, interleaved(i), :] + rb_s[lb, interleaved(i), :]
            o_ref[pl.ds(off, ts), lanes(lb)] = (rec * jax.nn.gelu(y)).astype(o_ref.dtype)
        return 0

    lax.fori_loop(0, n_ts, emit, 0, unroll=math.gcd(n_ts, 8))


def _lru(xr, yr, conv_w, conv_b, wg, lru_lambda):
    B, S, _ = xr.shape
    width = LRU_LB * LANES
    chunk = S // SUBLANES
    ts = min(LRU_TS, chunk)
    assert S % SUBLANES == 0 and chunk % ts == 0 and chunk % SCAN_GROUP == 0 and LRU_WIDTH % width == 0
    col = lambda: pl.BlockSpec((None, S, width), lambda b, c: (b, 0, c))
    seq = lambda: pltpu.VMEM((LRU_LB, S, LANES), F32)
    return pl.pallas_call(
        functools.partial(_lru_kernel, ts=ts),
        grid=(B, LRU_WIDTH // width),
        in_specs=[
            col(),
            col(),
            pl.BlockSpec((CONV_WIDTH, width), lambda b, c: (0, c)),
            pl.BlockSpec((1, width), lambda b, c: (0, c)),
            pl.BlockSpec((LRU_LB, 2 * LANES, 4 * LANES), lambda b, c: (c, 0, 0)),
            pl.BlockSpec((2, width), lambda b, c: (0, c)),
        ],
        out_specs=col(),
        out_shape=jax.ShapeDtypeStruct((B, S, LRU_WIDTH), BF16),
        scratch_shapes=[
            pltpu.VMEM((LRU_LB, S + 2 * SUBLANES, LANES), F32),
            seq(), seq(), seq(), seq(), seq(), seq(),
        ],
        compiler_params=_cparams(2),
        name="lru",
    )(xr, yr, conv_w, conv_b.reshape(1, LRU_WIDTH), wg, lru_lambda)


def _pack_gate_weights(w_gates, b_gates):
    ncb = LRU_WIDTH // LANES
    per = LANES // LRU_BLOCK
    w = w_gates.reshape(4, ncb, per, LRU_BLOCK, LRU_BLOCK)
    eye = jnp.eye(per, dtype=w.dtype)
    dense = jnp.einsum("gcpde,pq->cpdgqe", w, eye)
    dense = (0.5 * dense.reshape(ncb, LANES, 4 * LANES)).astype(BF16)
    bias = 0.5 * b_gates.reshape(4, ncb, LANES).transpose(1, 0, 2).reshape(ncb, 1, 4 * LANES)
    bias_hi, bias_lo = _split_bf16(bias)
    pad = jnp.zeros((ncb, LANES - 2, 4 * LANES), BF16)
    return jnp.concatenate([dense, bias_hi, bias_lo, pad], axis=1)


OUT_ROWS = 256


def _out_kernel(x_ref, attn_ref, rec_ref, ga_ref, gl_ref, mod_ref, wab_ref, wlb_ref, wo_ref,
                wfi_ref, wfo_ref, ln_ref, y_ref):
    tm = x_ref.shape[0]
    groups = [slice(g * OUT_ROWS, (g + 1) * OUT_ROWS) for g in range(tm // OUT_ROWS)]

    def branches(rows):
        a = jnp.dot(attn_ref[rows, :], wab_ref[...], preferred_element_type=F32)
        r = jnp.dot(rec_ref[rows, :], wlb_ref[...], preferred_element_type=F32)
        merged = _sigmoid(ga_ref[rows, :].astype(F32)) * a + _sigmoid(gl_ref[rows, :].astype(F32)) * r
        return merged.astype(BF16)

    def mixer_norm(rows, merged):
        mix = jnp.dot(merged, wo_ref[...], preferred_element_type=F32)
        x1 = _layer_norm(ALPHA * x_ref[rows, :] + mod_ref[2:3, :] * mix) * ln_ref[0:1, :] + ln_ref[1:2, :]
        h = _layer_norm(x1) * (1.0 + mod_ref[4:5, :]) + mod_ref[3:4, :]
        return x1, h.astype(BF16)

    def ffn_hidden(h):
        gu = jnp.dot(h, wfi_ref[...], preferred_element_type=F32)
        gate = gu[:, :D_FF]
        return (gate * _sigmoid(gate) * gu[:, D_FF:]).astype(BF16)

    def ffn_norm(rows, x1, act):
        f = jnp.dot(act, wfo_ref[...], preferred_element_type=F32)
        y_ref[rows, :] = _layer_norm(ALPHA * x1 + mod_ref[5:6, :] * f) * ln_ref[2:3, :] + ln_ref[3:4, :]

    merged = [branches(rows) for rows in groups]
    normed = [mixer_norm(rows, m) for rows, m in zip(groups, merged)]
    acts = [ffn_hidden(h) for _, h in normed]
    for rows, (x1, _), act in zip(groups, normed, acts):
        ffn_norm(rows, x1, act)


def _out(x, attn, rec, ga, gl, mod, wab, wlb, wo, wfi, wfo, ln):
    B, S, _ = x.shape
    tm = min(2 * OUT_ROWS, S)
    assert S % tm == 0 and tm % OUT_ROWS == 0
    tok = lambda: pl.BlockSpec((None, tm, D_MODEL), lambda b, i: (b, i, 0))
    return pl.pallas_call(
        _out_kernel,
        grid=(B, S // tm),
        in_specs=[
            tok(), tok(), tok(), tok(), tok(),
            pl.BlockSpec((None, 6, D_MODEL), lambda b, i: (b, 0, 0)),
            _resident(wab.shape), _resident(wlb.shape), _resident(wo.shape),
            _resident(wfi.shape), _resident(wfo.shape),
            _resident(ln.shape),
        ],
        out_specs=tok(),
        out_shape=jax.ShapeDtypeStruct((B, S, D_MODEL), F32),
        compiler_params=_cparams(2),
        name="out",
    )(x, attn, rec, ga, gl, mod, wab, wlb, wo, wfi, wfo, ln)


def _rope_tables(seq):
    inv = 1.0 / (ROPE_THETA ** (jnp.arange(0, HEAD_DIM, 2, dtype=F32) / HEAD_DIM))
    ang = jnp.arange(seq, dtype=F32)[:, None] * inv[None, :]
    cos, sin = jnp.cos(ang), jnp.sin(ang)
    return (jnp.concatenate([cos, cos, cos, cos], axis=1),
            jnp.concatenate([-sin, -sin, sin, sin], axis=1))


def _permute_qk_columns(w):
    d = w.shape[0]
    return w.reshape(d, N_HEADS, 2, 2, HEAD_DIM // 2).transpose(0, 1, 3, 2, 4).reshape(d, N_HEADS * LANES)


def kernel(x_prompt, x_sample, c_prompt, c_sample, w_ada, b_ada, w_in, lambda_q1, lambda_k1, lambda_q2, lambda_k2, subln_g, conv_w, conv_b, w_lru_gates, b_lru_gates, lru_lambda, w_attn_branch, w_lru_branch, w_out, ln1_g, ln1_b, w_ffn_in, w_ffn_out, ln2_g, ln2_b):
    assert w_ada.shape[0] == DEPTH == 1
    lambda_init = 0.8 - 0.6 * math.exp(-0.3 * 0)

    w_in0 = w_in[0]
    w_in_b = jnp.concatenate(
        [_permute_qk_columns(w_in0[:, :D_MODEL]), _permute_qk_columns(w_in0[:, D_MODEL:2 * D_MODEL]),
         w_in0[:, 2 * D_MODEL:]], axis=1).astype(BF16)
    lam_vecs = jnp.concatenate([lambda_q1, lambda_k1, lambda_q2, lambda_k2], axis=0).astype(F32)
    wg = _pack_gate_weights(w_lru_gates[0], b_lru_gates[0])
    wab = w_attn_branch[0].astype(BF16)
    wlb = w_lru_branch[0].astype(BF16)
    wo = w_out[0].astype(BF16)
    wfi = w_ffn_in[0].astype(BF16)
    wfo = w_ffn_out[0].astype(BF16)
    ln = jnp.concatenate([ln1_g, ln1_b, ln2_g, ln2_b], axis=0).astype(F32)

    nb = c_prompt.shape[0]
    mod_all = _ada(jnp.concatenate([c_prompt, c_sample], axis=0), w_ada[0], b_ada[0])
    mod_all = mod_all.reshape(mod_all.shape[0], 6, D_MODEL)

    def trunk(x, mod):
        cos_t, sin_t = _rope_tables(x.shape[1])
        q, k, v, xr, yr, ga, gl = _inproj(x, mod, cos_t, sin_t, w_in_b)
        attn = _attention(q, k, v, lam_vecs, subln_g[0].astype(F32), lambda_init)
        rec = _lru(xr, yr, conv_w[0], conv_b[0], wg, lru_lambda[0])
        return _out(x, attn, rec, ga, gl, mod, wab, wlb, wo, wfi, wfo, ln)

    return (trunk(x_prompt, mod_all[:nb]), trunk(x_sample, mod_all[nb:]))
```

```python
import functools
import math

import jax
import jax.numpy as jnp
from jax import lax
from jax.experimental import pallas as pl
from jax.experimental.pallas import tpu as pltpu

D_MODEL = 1024
N_HEADS = 8
HEAD_DIM = 64
V_DIM = 2 * HEAD_DIM
ROPE_THETA = 10000.0
RMS_EPS = 1e-5
LRU_WIDTH = 1024
LRU_BLOCKS = 16
LRU_BLOCK = LRU_WIDTH // LRU_BLOCKS
LRU_C = 8.0
CONV_WIDTH = 4
CONV_LEFT = 2
D_FF = 2816
DEPTH = 1
ALPHA = (2.0 * DEPTH) ** 0.25
LN_EPS = 1e-5
LOG2E = 1.4426950408889634

LANES = 128
SUBLANES = 8
VMEM_LIMIT = 56 * 1024 * 1024

F32 = jnp.float32
BF16 = jnp.bfloat16


def _cparams(n_axes):
    return pltpu.CompilerParams(
        dimension_semantics=("arbitrary",) * n_axes,
        vmem_limit_bytes=VMEM_LIMIT,
    )


def _resident(shape):
    nd = len(shape)
    return pl.BlockSpec(shape, lambda *_: (0,) * nd, pipeline_mode=pl.Buffered(1))


def _layer_norm(x):
    mu = jnp.mean(x, axis=-1, keepdims=True)
    xc = x - mu
    var = jnp.mean(xc * xc, axis=-1, keepdims=True)
    return xc * lax.rsqrt(var + LN_EPS)


def _sigmoid(x):
    return 0.5 * jnp.tanh(0.5 * x) + 0.5


def _split_bf16(x):
    hi = x.astype(BF16)
    lo = (x - hi.astype(F32)).astype(BF16)
    return hi, lo


def _ada_kernel(c_ref, w_ref, b_ref, o_ref):
    c = c_ref[...]
    s_hi, s_lo = _split_bf16(c * _sigmoid(c))
    w_hi, w_lo = _split_bf16(w_ref[...])
    acc = jnp.dot(s_hi, w_hi, preferred_element_type=F32)
    acc += jnp.dot(s_lo, w_hi, preferred_element_type=F32)
    acc += jnp.dot(s_hi, w_lo, preferred_element_type=F32)
    o_ref[...] = acc + b_ref[...]


def _ada(c, w_ada, b_ada):
    rows = c.shape[0]
    n_out = w_ada.shape[1]
    tn = 1024
    return pl.pallas_call(
        _ada_kernel,
        grid=(n_out // tn,),
        in_specs=[
            pl.BlockSpec((rows, D_MODEL), lambda j: (0, 0)),
            pl.BlockSpec((D_MODEL, tn), lambda j: (0, j)),
            pl.BlockSpec((1, tn), lambda j: (0, j)),
        ],
        out_specs=pl.BlockSpec((rows, tn), lambda j: (0, j)),
        out_shape=jax.ShapeDtypeStruct((rows, n_out), F32),
        compiler_params=_cparams(1),
        name="ada",
    )(c, w_ada, b_ada.reshape(1, n_out))


N_PROJ = 7
IN_ROWS = 256


def _inproj_kernel(x_ref, mod_ref, cos_ref, sin_ref, w_ref, *out_refs):
    tm = x_ref.shape[0]
    groups = [slice(r * IN_ROWS, (r + 1) * IN_ROWS) for r in range(tm // IN_ROWS)]
    hbs = []
    for rows in groups:
        h = _layer_norm(x_ref[rows, :]) * (1.0 + mod_ref[1:2, :]) + mod_ref[0:1, :]
        hbs.append(h.astype(BF16))
    q_scale = HEAD_DIM ** -0.5 * LOG2E
    for g, o_ref in enumerate(out_refs):
        for rows, hb in zip(groups, hbs):
            p = jnp.dot(hb, w_ref[:, g * D_MODEL:(g + 1) * D_MODEL], preferred_element_type=F32)
            if g < 2:
                cos = cos_ref[rows, :]
                sin = sin_ref[rows, :]
                for hh in range(N_HEADS):
                    t = p[:, hh * LANES:(hh + 1) * LANES]
                    r = t * cos + pltpu.roll(t, LANES // 2, axis=1) * sin
                    if g == 0:
                        r = r * q_scale
                    o_ref[rows, hh * LANES:(hh + 1) * LANES] = r.astype(o_ref.dtype)
            else:
                o_ref[rows, :] = p.astype(o_ref.dtype)


def _inproj(x, mod, cos_t, sin_t, w_in_b):
    B, S, _ = x.shape
    tm = min(2 * IN_ROWS, S)
    assert S % tm == 0 and tm % IN_ROWS == 0
    tok = lambda: pl.BlockSpec((None, tm, D_MODEL), lambda b, i: (b, i, 0))
    return pl.pallas_call(
        _inproj_kernel,
        grid=(B, S // tm),
        in_specs=[
            tok(),
            pl.BlockSpec((None, 6, D_MODEL), lambda b, i: (b, 0, 0)),
            pl.BlockSpec((tm, LANES), lambda b, i: (i, 0)),
            pl.BlockSpec((tm, LANES), lambda b, i: (i, 0)),
            _resident(w_in_b.shape),
        ],
        out_specs=[tok() for _ in range(N_PROJ)],
        out_shape=[jax.ShapeDtypeStruct((B, S, D_MODEL), BF16) for _ in range(N_PROJ)],
        compiler_params=_cparams(2),
        name="inproj",
    )(x, mod, cos_t, sin_t, w_in_b)


ONES_ROWS = 16


def _attn_kernel(lam_ref, g_ref, q_ref, k_ref, v_ref, o_ref, vt_ref, s_ref, acc_ref, *, tq, kc, lambda_init):
    S = k_ref.shape[0]
    nq = S // tq
    nc = S // kc

    vt_ref[0:V_DIM, :] = v_ref[...].astype(F32).T.astype(BF16)
    vt_ref[V_DIM:V_DIM + ONES_ROWS, :] = jnp.ones((ONES_ROWS, S), BF16)

    lv = lam_ref[...]
    lam = (jnp.exp(jnp.sum(lv[0:1] * lv[1:2], axis=1, keepdims=True))
           - jnp.exp(jnp.sum(lv[2:3] * lv[3:4], axis=1, keepdims=True)) + lambda_init)

    lane = lax.broadcasted_iota(jnp.int32, (tq, LANES), 1)
    is_map0 = (lane % (2 * 32)) < 32

    def masked_queries(i):
        q = q_ref[pl.ds(pl.multiple_of(i * tq, tq), tq), :]
        zero = jnp.zeros_like(q)
        return jnp.concatenate([jnp.where(is_map0, q, zero), jnp.where(is_map0, zero, q)], axis=0)

    def score_chunk(qm, slot, c, m):
        s = lax.dot_general(k_ref[c * kc:(c + 1) * kc, :], qm, (((1,), (1,)), ((), ())),
                            preferred_element_type=F32)
        s_ref[slot, c * kc:(c + 1) * kc, :] = s
        m_c = jnp.max(s, axis=0, keepdims=True)
        return m_c if m is None else jnp.maximum(m, m_c)

    def value_chunk(slot, c, m):
        e = jnp.exp2(s_ref[slot, c * kc:(c + 1) * kc, :] - m).astype(BF16)
        part = jnp.dot(vt_ref[:, c * kc:(c + 1) * kc], e, preferred_element_type=F32)
        if c == 0:
            acc_ref[slot] = part
        else:
            acc_ref[slot] += part

    def finish(i, slot):
        acc = acc_ref[slot]
        ot = acc[0:V_DIM] * (1.0 / acc[V_DIM:V_DIM + 1])
        o = ot[:, :tq] - lam * ot[:, tq:]
        o = o * lax.rsqrt(jnp.mean(o * o, axis=0, keepdims=True) + RMS_EPS)
        o = o * g_ref[...] * (1.0 - lambda_init)
        o_ref[pl.ds(pl.multiple_of(i * tq, tq), tq), :] = o.T.astype(o_ref.dtype)

    def scores_only(i, slot):
        qm = masked_queries(i)
        m = None
        for c in range(nc):
            m = score_chunk(qm, slot, c, m)
        return m

    def value_phase(i, slot, m, with_next_scores, finish_previous):
        qm_next = masked_queries(i + 1) if with_next_scores else None
        m_next = None
        for c in range(nc):
            if with_next_scores:
                m_next = score_chunk(qm_next, 1 - slot, c, m_next)
            value_chunk(slot, c, m)
            if c == 0 and finish_previous:
                finish(i - 1, 1 - slot)
        return m_next

    m_even = scores_only(0, 0)
    if nq == 1:
        value_phase(0, 0, m_even, False, False)
        finish(0, 0)
    else:
        m_odd = value_phase(0, 0, m_even, True, False)

        def pair(j, m):
            m_e = value_phase(2 * j + 1, 1, m, True, True)
            return value_phase(2 * j + 2, 0, m_e, True, True)

        m_odd = lax.fori_loop(0, nq // 2 - 1, pair, m_odd)
        value_phase(nq - 1, 1, m_odd, False, True)
        finish(nq - 1, 1)


def _attention(q, k, v, lam_vecs, subln_g, lambda_init):
    B, S, _ = q.shape
    tq = min(256, S)
    kc = min(512, S)
    nq = S // tq
    assert S % tq == 0 and S % kc == 0 and (nq == 1 or nq % 2 == 0)
    col = lambda: pl.BlockSpec((None, S, LANES), lambda b, h: (b, 0, h))
    return pl.pallas_call(
        functools.partial(_attn_kernel, tq=tq, kc=kc, lambda_init=lambda_init),
        grid=(B, N_HEADS),
        in_specs=[
            pl.BlockSpec((4, HEAD_DIM), lambda b, h: (0, 0)),
            pl.BlockSpec((V_DIM, 1), lambda b, h: (0, 0)),
            col(), col(), col(),
        ],
        out_specs=col(),
        out_shape=jax.ShapeDtypeStruct((B, S, D_MODEL), BF16),
        scratch_shapes=[
            pltpu.VMEM((V_DIM + ONES_ROWS, S), BF16),
            pltpu.VMEM((2, S, 2 * tq), F32),
            pltpu.VMEM((2, V_DIM + ONES_ROWS, 2 * tq), F32),
        ],
        compiler_params=_cparams(2),
        name="attn",
    )(lam_vecs, subln_g.reshape(V_DIM, 1), q, k, v)


LRU_TS = 256
LRU_LB = 2
SCAN_GROUP = 16


def _lru_kernel(xr_ref, yr_ref, cw_ref, cb_ref, wg_ref, lam_ref, o_ref,
                x_s, af_s, df_s, ab_s, db_s, rf_s, rb_s, *, ts):
    S = xr_ref.shape[0]
    chunk = S // SUBLANES
    n_ts = S // ts
    blocks = range(LRU_LB)

    def lanes(lb):
        return slice(lb * LANES, (lb + 1) * LANES)

    def interleaved(i):
        first = i * ts
        return pl.ds((first % chunk) * SUBLANES + first // chunk, ts, stride=SUBLANES)

    for lb in blocks:
        x_s[lb, 0:SUBLANES, :] = jnp.zeros((SUBLANES, LANES), F32)
        x_s[lb, S + SUBLANES:S + 2 * SUBLANES, :] = jnp.zeros((SUBLANES, LANES), F32)

    def stage(i, _):
        off = pl.multiple_of(i * ts, ts)
        for lb in blocks:
            x_s[lb, pl.ds(off + SUBLANES, ts), :] = xr_ref[pl.ds(off, ts), lanes(lb)].astype(F32)
        return 0

    lax.fori_loop(0, n_ts, stage, 0, unroll=math.gcd(n_ts, 4))

    cw = cw_ref[...]
    cb = cb_ref[...]
    z = -lam_ref[...]
    softplus = jnp.maximum(z, 0.0) + jnp.log1p(jnp.exp(-jnp.abs(z)))
    half_neg_c_sp = (-0.5 * LRU_C) * softplus
    lane = lax.broadcasted_iota(jnp.int32, (ts, LANES), 1)
    bias_lanes = jnp.where(lane < 2, 1.0, 0.0).astype(BF16)
    tiny = float(jnp.finfo(F32).tiny)

    def gates(i, _):
        off = pl.multiple_of(i * ts, ts)
        for lb in blocks:
            xc = cb[:, lanes(lb)]
            for tap in range(CONV_WIDTH):
                xc = xc + cw[tap:tap + 1, lanes(lb)] * x_s[lb, pl.ds(off + SUBLANES + tap - CONV_LEFT, ts), :]
            lhs = jnp.concatenate([xc.astype(BF16), bias_lanes], axis=1)
            g = jnp.dot(lhs, wg_ref[lb], preferred_element_type=F32)
            sx = -math.sqrt(0.5) * xc
            for d, (a_s, d_s) in enumerate(((af_s, df_s), (ab_s, db_s))):
                k = half_neg_c_sp[d:d + 1, lanes(lb)]
                log_a = k * jnp.tanh(g[:, (2 * d) * LANES:(2 * d + 1) * LANES]) + k
                a_s[lb, interleaved(i), :] = jnp.exp(log_a)
                gated_x = sx * jnp.tanh(g[:, (2 * d + 1) * LANES:(2 * d + 2) * LANES]) + sx
                th = jnp.tanh(log_a)
                root = th * lax.rsqrt(jnp.maximum(th * (th - 1.0), tiny))
                d_s[lb, interleaved(i), :] = root * gated_x
        return 0

    lax.fori_loop(0, n_ts, gates, 0, unroll=math.gcd(n_ts, 8))

    zeros = jnp.zeros((SUBLANES, LANES), F32)
    ones = jnp.ones((SUBLANES, LANES), F32)

    n_groups = chunk // SCAN_GROUP
    group_rows = SCAN_GROUP * SUBLANES

    def group(g):
        return pl.ds(pl.multiple_of(g * group_rows, group_rows), group_rows)

    def scan_group(a, d, order, h, p):
        states = {}
        for k0, k1 in zip(order[0::2], order[1::2]):
            a0 = a[k0 * SUBLANES:(k0 + 1) * SUBLANES]
            d0 = d[k0 * SUBLANES:(k0 + 1) * SUBLANES]
            a1 = a[k1 * SUBLANES:(k1 + 1) * SUBLANES]
            a2 = a1 * a0
            d2 = a1 * d0 + d[k1 * SUBLANES:(k1 + 1) * SUBLANES]
            states[k0] = a0 * h + d0
            h = a2 * h + d2
            states[k1] = h
            p = a2 * p
        return h, p, states

    forward = list(range(SCAN_GROUP))
    backward = forward[::-1]

    def local_scan(g, carry):
        gb = n_groups - 1 - g
        out = []
        for lb, (hf, pf, hb, pb) in zip(blocks, carry):
            hf, pf, _ = scan_group(af_s[lb, group(g), :], df_s[lb, group(g), :], forward, hf, pf)
            hb, pb, _ = scan_group(ab_s[lb, group(gb), :], db_s[lb, group(gb), :], backward, hb, pb)
            out.append((hf, pf, hb, pb))
        return tuple(out)

    ends = lax.fori_loop(0, n_groups, local_scan, tuple((zeros, ones, zeros, ones) for _ in blocks))

    row = lax.broadcasted_iota(jnp.int32, (SUBLANES, LANES), 0)

    def pick(v, s):
        return jnp.sum(jnp.where(row == s, v, 0.0), axis=0, keepdims=True)

    entering = []
    for hf_last, pf_last, hb_first, pb_first in ends:
        cf = zeros
        for s in range(1, SUBLANES):
            nxt = pick(hf_last, s - 1) + pick(pf_last, s - 1) * pick(cf, s - 1)
            cf = jnp.where(row == s, nxt, cf)
        cbk = zeros
        for s in range(SUBLANES - 2, -1, -1):
            nxt = pick(hb_first, s + 1) + pick(pb_first, s + 1) * pick(cbk, s + 1)
            cbk = jnp.where(row == s, nxt, cbk)
        entering.append((cf, cbk))

    def full_scan(g, carry):
        gb = n_groups - 1 - g
        out = []
        for lb, (hf, hb) in zip(blocks, carry):
            hf, _, states = scan_group(af_s[lb, group(g), :], df_s[lb, group(g), :], forward, hf, ones)
            rf_s[lb, group(g), :] = jnp.concatenate([states[k] for k in forward], axis=0)
            hb, _, states = scan_group(ab_s[lb, group(gb), :], db_s[lb, group(gb), :], backward, hb, ones)
            rb_s[lb, group(gb), :] = jnp.concatenate([states[k] for k in forward], axis=0)
            out.append((hf, hb))
        return tuple(out)

    lax.fori_loop(0, n_groups, full_scan, tuple(entering))

    def emit(i, _):
        off = pl.multiple_of(i * ts, ts)
        for lb in blocks:
            y = yr_ref[pl.ds(off, ts), lanes(lb)].astype(F32)
            rec = rf_s[lb, interleaved(i), :] + rb_s[lb, interleaved(i), :]
            o_ref[pl.ds(off, ts), lanes(lb)] = (rec * jax.nn.gelu(y)).astype(o_ref.dtype)
        return 0

    lax.fori_loop(0, n_ts, emit, 0, unroll=math.gcd(n_ts, 8))


def _lru(xr, yr, conv_w, conv_b, wg, lru_lambda):
    B, S, _ = xr.shape
    width = LRU_LB * LANES
    chunk = S // SUBLANES
    ts = min(LRU_TS, chunk)
    assert S % SUBLANES == 0 and chunk % ts == 0 and chunk % SCAN_GROUP == 0 and LRU_WIDTH % width == 0
    col = lambda: pl.BlockSpec((None, S, width), lambda b, c: (b, 0, c))
    seq = lambda: pltpu.VMEM((LRU_LB, S, LANES), F32)
    return pl.pallas_call(
        functools.partial(_lru_kernel, ts=ts),
        grid=(B, LRU_WIDTH // width),
        in_specs=[
            col(),
            col(),
            pl.BlockSpec((CONV_WIDTH, width), lambda b, c: (0, c)),
            pl.BlockSpec((1, width), lambda b, c: (0, c)),
            pl.BlockSpec((LRU_LB, 2 * LANES, 4 * LANES), lambda b, c: (c, 0, 0)),
            pl.BlockSpec((2, width), lambda b, c: (0, c)),
        ],
        out_specs=col(),
        out_shape=jax.ShapeDtypeStruct((B, S, LRU_WIDTH), BF16),
        scratch_shapes=[
            pltpu.VMEM((LRU_LB, S + 2 * SUBLANES, LANES), F32),
            seq(), seq(), seq(), seq(), seq(), seq(),
        ],
        compiler_params=_cparams(2),
        name="lru",
    )(xr, yr, conv_w, conv_b.reshape(1, LRU_WIDTH), wg, lru_lambda)


def _pack_gate_weights(w_gates, b_gates):
    ncb = LRU_WIDTH // LANES
    per = LANES // LRU_BLOCK
    w = w_gates.reshape(4, ncb, per, LRU_BLOCK, LRU_BLOCK)
    eye = jnp.eye(per, dtype=w.dtype)
    dense = jnp.einsum("gcpde,pq->cpdgqe", w, eye)
    dense = (0.5 * dense.reshape(ncb, LANES, 4 * LANES)).astype(BF16)
    bias = 0.5 * b_gates.reshape(4, ncb, LANES).transpose(1, 0, 2).reshape(ncb, 1, 4 * LANES)
    bias_hi, bias_lo = _split_bf16(bias)
    pad = jnp.zeros((ncb, LANES - 2, 4 * LANES), BF16)
    return jnp.concatenate([dense, bias_hi, bias_lo, pad], axis=1)


OUT_ROWS = 256


def _out_kernel(x_ref, attn_ref, rec_ref, ga_ref, gl_ref, mod_ref, wab_ref, wlb_ref, wo_ref,
                wfi_ref, wfo_ref, ln_ref, y_ref):
    tm = x_ref.shape[0]
    groups = [slice(g * OUT_ROWS, (g + 1) * OUT_ROWS) for g in range(tm // OUT_ROWS)]

    def branches(rows):
        a = jnp.dot(attn_ref[rows, :], wab_ref[...], preferred_element_type=F32)
        r = jnp.dot(rec_ref[rows, :], wlb_ref[...], preferred_element_type=F32)
        merged = _sigmoid(ga_ref[rows, :].astype(F32)) * a + _sigmoid(gl_ref[rows, :].astype(F32)) * r
        return merged.astype(BF16)

    def mixer_norm(rows, merged):
        mix = jnp.dot(merged, wo_ref[...], preferred_element_type=F32)
        x1 = _layer_norm(ALPHA * x_ref[rows, :] + mod_ref[2:3, :] * mix) * ln_ref[0:1, :] + ln_ref[1:2, :]
        h = _layer_norm(x1) * (1.0 + mod_ref[4:5, :]) + mod_ref[3:4, :]
        return x1, h.astype(BF16)

    def ffn_hidden(h):
        gu = jnp.dot(h, wfi_ref[...], preferred_element_type=F32)
        gate = gu[:, :D_FF]
        return (gate * _sigmoid(gate) * gu[:, D_FF:]).astype(BF16)

    def ffn_norm(rows, x1, act):
        f = jnp.dot(act, wfo_ref[...], preferred_element_type=F32)
        y_ref[rows, :] = _layer_norm(ALPHA * x1 + mod_ref[5:6, :] * f) * ln_ref[2:3, :] + ln_ref[3:4, :]

    merged = [branches(rows) for rows in groups]
    normed = [mixer_norm(rows, m) for rows, m in zip(groups, merged)]
    acts = [ffn_hidden(h) for _, h in normed]
    for rows, (x1, _), act in zip(groups, normed, acts):
        ffn_norm(rows, x1, act)


def _out(x, attn, rec, ga, gl, mod, wab, wlb, wo, wfi, wfo, ln):
    B, S, _ = x.shape
    tm = min(2 * OUT_ROWS, S)
    assert S % tm == 0 and tm % OUT_ROWS == 0
    tok = lambda: pl.BlockSpec((None, tm, D_MODEL), lambda b, i: (b, i, 0))
    return pl.pallas_call(
        _out_kernel,
        grid=(B, S // tm),
        in_specs=[
            tok(), tok(), tok(), tok(), tok(),
            pl.BlockSpec((None, 6, D_MODEL), lambda b, i: (b, 0, 0)),
            _resident(wab.shape), _resident(wlb.shape), _resident(wo.shape),
            _resident(wfi.shape), _resident(wfo.shape),
            _resident(ln.shape),
        ],
        out_specs=tok(),
        out_shape=jax.ShapeDtypeStruct((B, S, D_MODEL), F32),
        compiler_params=_cparams(2),
        name="out",
    )(x, attn, rec, ga, gl, mod, wab, wlb, wo, wfi, wfo, ln)


def _rope_tables(seq):
    inv = 1.0 / (ROPE_THETA ** (jnp.arange(0, HEAD_DIM, 2, dtype=F32) / HEAD_DIM))
    ang = jnp.arange(seq, dtype=F32)[:, None] * inv[None, :]
    cos, sin = jnp.cos(ang), jnp.sin(ang)
    return (jnp.concatenate([cos, cos, cos, cos], axis=1),
            jnp.concatenate([-sin, -sin, sin, sin], axis=1))


def _permute_qk_columns(w):
    d = w.shape[0]
    return w.reshape(d, N_HEADS, 2, 2, HEAD_DIM // 2).transpose(0, 1, 3, 2, 4).reshape(d, N_HEADS * LANES)


def kernel(x_prompt, x_sample, c_prompt, c_sample, w_ada, b_ada, w_in, lambda_q1, lambda_k1, lambda_q2, lambda_k2, subln_g, conv_w, conv_b, w_lru_gates, b_lru_gates, lru_lambda, w_attn_branch, w_lru_branch, w_out, ln1_g, ln1_b, w_ffn_in, w_ffn_out, ln2_g, ln2_b):
    assert w_ada.shape[0] == DEPTH == 1
    lambda_init = 0.8 - 0.6 * math.exp(-0.3 * 0)

    w_in0 = w_in[0]
    w_in_b = jnp.concatenate(
        [_permute_qk_columns(w_in0[:, :D_MODEL]), _permute_qk_columns(w_in0[:, D_MODEL:2 * D_MODEL]),
         w_in0[:, 2 * D_MODEL:]], axis=1).astype(BF16)
    lam_vecs = jnp.concatenate([lambda_q1, lambda_k1, lambda_q2, lambda_k2], axis=0).astype(F32)
    wg = _pack_gate_weights(w_lru_gates[0], b_lru_gates[0])
    wab = w_attn_branch[0].astype(BF16)
    wlb = w_lru_branch[0].astype(BF16)
    wo = w_out[0].astype(BF16)
    wfi = w_ffn_in[0].astype(BF16)
    wfo = w_ffn_out[0].astype(BF16)
    ln = jnp.concatenate([ln1_g, ln1_b, ln2_g, ln2_b], axis=0).astype(F32)

    nb = c_prompt.shape[0]
    mod_all = _ada(jnp.concatenate([c_prompt, c_sample], axis=0), w_ada[0], b_ada[0])
    mod_all = mod_all.reshape(mod_all.shape[0], 6, D_MODEL)

    def trunk(x, mod):
        cos_t, sin_t = _rope_tables(x.shape[1])
        q, k, v, xr, yr, ga, gl = _inproj(x, mod, cos_t, sin_t, w_in_b)
        attn = _attention(q, k, v, lam_vecs, subln_g[0].astype(F32), lambda_init)
        rec = _lru(xr, yr, conv_w[0], conv_b[0], wg, lru_lambda[0])
        return _out(x, attn, rec, ga, gl, mod, wab, wlb, wo, wfi, wfo, ln)

    return (trunk(x_prompt, mod_all[:nb]), trunk(x_sample, mod_all[nb:]))
```

```python
import functools
import math

import jax
import jax.numpy as jnp
from jax import lax
from jax.experimental import pallas as pl
from jax.experimental.pallas import tpu as pltpu

D_MODEL = 1024
N_HEADS = 8
HEAD_DIM = 64
V_DIM = 2 * HEAD_DIM
ROPE_THETA = 10000.0
RMS_EPS = 1e-5
LRU_WIDTH = 1024
LRU_BLOCKS = 16
LRU_BLOCK = LRU_WIDTH // LRU_BLOCKS
LRU_C = 8.0
CONV_WIDTH = 4
CONV_LEFT = 2
D_FF = 2816
DEPTH = 1
ALPHA = (2.0 * DEPTH) ** 0.25
LN_EPS = 1e-5
LOG2E = 1.4426950408889634

LANES = 128
SUBLANES = 8
VMEM_LIMIT = 56 * 1024 * 1024

F32 = jnp.float32
BF16 = jnp.bfloat16


def _cparams(n_axes):
    return pltpu.CompilerParams(
        dimension_semantics=("arbitrary",) * n_axes,
        vmem_limit_bytes=VMEM_LIMIT,
    )


def _resident(shape):
    nd = len(shape)
    return pl.BlockSpec(shape, lambda *_: (0,) * nd, pipeline_mode=pl.Buffered(1))


def _layer_norm(x):
    mu = jnp.mean(x, axis=-1, keepdims=True)
    xc = x - mu
    var = jnp.mean(xc * xc, axis=-1, keepdims=True)
    return xc * lax.rsqrt(var + LN_EPS)


def _sigmoid(x):
    return 0.5 * jnp.tanh(0.5 * x) + 0.5


def _split_bf16(x):
    hi = x.astype(BF16)
    lo = (x - hi.astype(F32)).astype(BF16)
    return hi, lo


def _ada_kernel(c_ref, w_ref, b_ref, o_ref):
    c = c_ref[...]
    s_hi, s_lo = _split_bf16(c * _sigmoid(c))
    w_hi, w_lo = _split_bf16(w_ref[...])
    acc = jnp.dot(s_hi, w_hi, preferred_element_type=F32)
    acc += jnp.dot(s_lo, w_hi, preferred_element_type=F32)
    acc += jnp.dot(s_hi, w_lo, preferred_element_type=F32)
    o_ref[...] = acc + b_ref[...]


def _ada(c, w_ada, b_ada):
    rows = c.shape[0]
    n_out = w_ada.shape[1]
    tn = 1024
    return pl.pallas_call(
        _ada_kernel,
        grid=(n_out // tn,),
        in_specs=[
            pl.BlockSpec((rows, D_MODEL), lambda j: (0, 0)),
            pl.BlockSpec((D_MODEL, tn), lambda j: (0, j)),
            pl.BlockSpec((1, tn), lambda j: (0, j)),
        ],
        out_specs=pl.BlockSpec((rows, tn), lambda j: (0, j)),
        out_shape=jax.ShapeDtypeStruct((rows, n_out), F32),
        compiler_params=_cparams(1),
        name="ada",
    )(c, w_ada, b_ada.reshape(1, n_out))


N_PROJ = 7
IN_ROWS = 256


def _inproj_kernel(x_ref, mod_ref, cos_ref, sin_ref, w_ref, *out_refs):
    tm = x_ref.shape[0]
    groups = [slice(r * IN_ROWS, (r + 1) * IN_ROWS) for r in range(tm // IN_ROWS)]
    hbs = []
    for rows in groups:
        h = _layer_norm(x_ref[rows, :]) * (1.0 + mod_ref[1:2, :]) + mod_ref[0:1, :]
        hbs.append(h.astype(BF16))
    q_scale = HEAD_DIM ** -0.5 * LOG2E
    for g, o_ref in enumerate(out_refs):
        for rows, hb in zip(groups, hbs):
            p = jnp.dot(hb, w_ref[:, g * D_MODEL:(g + 1) * D_MODEL], preferred_element_type=F32)
            if g < 2:
                cos = cos_ref[rows, :]
                sin = sin_ref[rows, :]
                for hh in range(N_HEADS):
                    t = p[:, hh * LANES:(hh + 1) * LANES]
                    r = t * cos + pltpu.roll(t, LANES // 2, axis=1) * sin
                    if g == 0:
                        r = r * q_scale
                    o_ref[rows, hh * LANES:(hh + 1) * LANES] = r.astype(o_ref.dtype)
            else:
                o_ref[rows, :] = p.astype(o_ref.dtype)


def _inproj(x, mod, cos_t, sin_t, w_in_b):
    B, S, _ = x.shape
    tm = min(2 * IN_ROWS, S)
    assert S % tm == 0 and tm % IN_ROWS == 0
    tok = lambda: pl.BlockSpec((None, tm, D_MODEL), lambda b, i: (b, i, 0))
    return pl.pallas_call(
        _inproj_kernel,
        grid=(B, S // tm),
        in_specs=[
            tok(),
            pl.BlockSpec((None, 6, D_MODEL), lambda b, i: (b, 0, 0)),
            pl.BlockSpec((tm, LANES), lambda b, i: (i, 0)),
            pl.BlockSpec((tm, LANES), lambda b, i: (i, 0)),
            _resident(w_in_b.shape),
        ],
        out_specs=[tok() for _ in range(N_PROJ)],
        out_shape=[jax.ShapeDtypeStruct((B, S, D_MODEL), BF16) for _ in range(N_PROJ)],
        compiler_params=_cparams(2),
        name="inproj",
    )(x, mod, cos_t, sin_t, w_in_b)


ONES_ROWS = 16


def _attn_kernel(lam_ref, g_ref, q_ref, k_ref, v_ref, qn_ref, kn_ref, o_ref, vt_ref, s_ref, acc_ref, m_ref,
                 *, tq, kc, lambda_init):
    S = k_ref.shape[0]
    nq = S // tq
    nc = S // kc

    vt_ref[0:V_DIM, :] = v_ref[...].astype(F32).T.astype(BF16)
    vt_ref[V_DIM:V_DIM + ONES_ROWS, :] = jnp.ones((ONES_ROWS, S), BF16)

    lv = lam_ref[...]
    lam = (jnp.exp(jnp.sum(lv[0:1] * lv[1:2], axis=1, keepdims=True))
           - jnp.exp(jnp.sum(lv[2:3] * lv[3:4], axis=1, keepdims=True)) + lambda_init)

    lane = lax.broadcasted_iota(jnp.int32, (tq, LANES), 1)
    is_map0 = (lane % (2 * 32)) < 32

    def mask_maps(q):
        zero = jnp.zeros_like(q)
        return jnp.concatenate([jnp.where(is_map0, q, zero), jnp.where(is_map0, zero, q)], axis=0)

    def masked_queries(i):
        return mask_maps(q_ref[pl.ds(pl.multiple_of(i * tq, tq), tq), :])

    def score_chunk(keys_ref, qm, slot, c, m):
        s = lax.dot_general(keys_ref[c * kc:(c + 1) * kc, :], qm, (((1,), (1,)), ((), ())),
                            preferred_element_type=F32)
        s_ref[slot, c * kc:(c + 1) * kc, :] = s
        m_c = jnp.max(s, axis=0, keepdims=True)
        return m_c if m is None else jnp.maximum(m, m_c)

    def value_chunk(slot, c, m):
        e = jnp.exp2(s_ref[slot, c * kc:(c + 1) * kc, :] - m).astype(BF16)
        part = jnp.dot(vt_ref[:, c * kc:(c + 1) * kc], e, preferred_element_type=F32)
        if c == 0:
            acc_ref[slot] = part
        else:
            acc_ref[slot] += part

    def finish(i, slot):
        acc = acc_ref[slot]
        ot = acc[0:V_DIM] * (1.0 / acc[V_DIM:V_DIM + 1])
        o = ot[:, :tq] - lam * ot[:, tq:]
        o = o * lax.rsqrt(jnp.mean(o * o, axis=0, keepdims=True) + RMS_EPS)
        o = o * g_ref[...] * (1.0 - lambda_init)
        o_ref[pl.ds(pl.multiple_of(i * tq, tq), tq), :] = o.T.astype(o_ref.dtype)

    def scores_only(i, slot):
        qm = masked_queries(i)
        m = None
        for c in range(nc):
            m = score_chunk(k_ref, qm, slot, c, m)
        return m

    NEXT_TILE, NEXT_STEP = "tile", "step"

    def value_phase(i, slot, m, next_scores, finish_previous):
        if next_scores == NEXT_TILE:
            keys_ref, qm_next = k_ref, masked_queries(i + 1)
        elif next_scores == NEXT_STEP:
            keys_ref, qm_next = kn_ref, mask_maps(qn_ref[...])
        m_next = None
        for c in range(nc):
            if next_scores is not None:
                m_next = score_chunk(keys_ref, qm_next, 1 - slot, c, m_next)
            value_chunk(slot, c, m)
            if c == 0 and finish_previous:
                finish(i - 1, 1 - slot)
        return m_next

    if nq == 1:
        value_phase(0, 0, scores_only(0, 0), None, False)
        finish(0, 0)
    else:
        @pl.when((pl.program_id(0) == 0) & (pl.program_id(1) == 0))
        def _():
            m_ref[...] = scores_only(0, 0)

        m_odd = value_phase(0, 0, m_ref[...], NEXT_TILE, False)

        def pair(j, m):
            m_e = value_phase(2 * j + 1, 1, m, NEXT_TILE, True)
            return value_phase(2 * j + 2, 0, m_e, NEXT_TILE, True)

        m_odd = lax.fori_loop(0, nq // 2 - 1, pair, m_odd)
        m_ref[...] = value_phase(nq - 1, 1, m_odd, NEXT_STEP, True)
        finish(nq - 1, 1)


def _attention(q, k, v, lam_vecs, subln_g, lambda_init):
    B, S, _ = q.shape
    tq = min(256, S)
    kc = min(512, S)
    nq = S // tq
    assert S % tq == 0 and S % kc == 0 and (nq == 1 or nq % 2 == 0)
    col = lambda: pl.BlockSpec((None, S, LANES), lambda b, h: (b, 0, h))

    def next_step(b, h):
        t = jnp.minimum(b * N_HEADS + h + 1, B * N_HEADS - 1)
        return t // N_HEADS, 0, t % N_HEADS

    return pl.pallas_call(
        functools.partial(_attn_kernel, tq=tq, kc=kc, lambda_init=lambda_init),
        grid=(B, N_HEADS),
        in_specs=[
            pl.BlockSpec((4, HEAD_DIM), lambda b, h: (0, 0)),
            pl.BlockSpec((V_DIM, 1), lambda b, h: (0, 0)),
            col(), col(), col(),
            pl.BlockSpec((None, tq, LANES), next_step),
            pl.BlockSpec((None, S, LANES), next_step),
        ],
        out_specs=col(),
        out_shape=jax.ShapeDtypeStruct((B, S, D_MODEL), BF16),
        scratch_shapes=[
            pltpu.VMEM((V_DIM + ONES_ROWS, S), BF16),
            pltpu.VMEM((2, S, 2 * tq), F32),
            pltpu.VMEM((2, V_DIM + ONES_ROWS, 2 * tq), F32),
            pltpu.VMEM((1, 2 * tq), F32),
        ],
        compiler_params=_cparams(2),
        name="attn",
    )(lam_vecs, subln_g.reshape(V_DIM, 1), q, k, v, q, k)


LRU_TS = 256
LRU_LB = 2
SCAN_GROUP = 16


def _lru_kernel(xr_ref, yr_ref, cw_ref, cb_ref, wg_ref, lam_ref, o_ref,
                x_s, af_s, df_s, ab_s, db_s, rf_s, rb_s, *, ts):
    S = xr_ref.shape[0]
    chunk = S // SUBLANES
    n_ts = S // ts
    blocks = range(LRU_LB)

    def lanes(lb):
        return slice(lb * LANES, (lb + 1) * LANES)

    def interleaved(i):
        first = i * ts
        return pl.ds((first % chunk) * SUBLANES + first // chunk, ts, stride=SUBLANES)

    for lb in blocks:
        x_s[lb, 0:SUBLANES, :] = jnp.zeros((SUBLANES, LANES), F32)
        x_s[lb, S + SUBLANES:S + 2 * SUBLANES, :] = jnp.zeros((SUBLANES, LANES), F32)

    def stage(i, _):
        off = pl.multiple_of(i * ts, ts)
        for lb in blocks:
            x_s[lb, pl.ds(off + SUBLANES, ts), :] = xr_ref[pl.ds(off, ts), lanes(lb)].astype(F32)
        return 0

    lax.fori_loop(0, n_ts, stage, 0, unroll=math.gcd(n_ts, 4))

    cw = cw_ref[...]
    cb = cb_ref[...]
    z = -lam_ref[...]
    softplus = jnp.maximum(z, 0.0) + jnp.log1p(jnp.exp(-jnp.abs(z)))
    half_neg_c_sp = (-0.5 * LRU_C) * softplus
    lane = lax.broadcasted_iota(jnp.int32, (ts, LANES), 1)
    bias_lanes = jnp.where(lane < 2, 1.0, 0.0).astype(BF16)
    tiny = float(jnp.finfo(F32).tiny)

    def gates(i, _):
        off = pl.multiple_of(i * ts, ts)
        for lb in blocks:
            xc = cb[:, lanes(lb)]
            for tap in range(CONV_WIDTH):
                xc = xc + cw[tap:tap + 1, lanes(lb)] * x_s[lb, pl.ds(off + SUBLANES + tap - CONV_LEFT, ts), :]
            lhs = jnp.concatenate([xc.astype(BF16), bias_lanes], axis=1)
            g = jnp.dot(lhs, wg_ref[lb], preferred_element_type=F32)
            sx = -math.sqrt(0.5) * xc
            for d, (a_s, d_s) in enumerate(((af_s, df_s), (ab_s, db_s))):
                k = half_neg_c_sp[d:d + 1, lanes(lb)]
                log_a = k * jnp.tanh(g[:, (2 * d) * LANES:(2 * d + 1) * LANES]) + k
                a_s[lb, interleaved(i), :] = jnp.exp(log_a)
                gated_x = sx * jnp.tanh(g[:, (2 * d + 1) * LANES:(2 * d + 2) * LANES]) + sx
                th = jnp.tanh(log_a)
                root = th * lax.rsqrt(jnp.maximum(th * (th - 1.0), tiny))
                d_s[lb, interleaved(i), :] = root * gated_x
        return 0

    lax.fori_loop(0, n_ts, gates, 0, unroll=math.gcd(n_ts, 8))

    zeros = jnp.zeros((SUBLANES, LANES), F32)
    ones = jnp.ones((SUBLANES, LANES), F32)

    n_groups = chunk // SCAN_GROUP
    group_rows = SCAN_GROUP * SUBLANES

    def group(g):
        return pl.ds(pl.multiple_of(g * group_rows, group_rows), group_rows)

    def scan_group(a, d, order, h, p):
        states = {}
        for k0, k1 in zip(order[0::2], order[1::2]):
            a0 = a[k0 * SUBLANES:(k0 + 1) * SUBLANES]
            d0 = d[k0 * SUBLANES:(k0 + 1) * SUBLANES]
            a1 = a[k1 * SUBLANES:(k1 + 1) * SUBLANES]
            a2 = a1 * a0
            d2 = a1 * d0 + d[k1 * SUBLANES:(k1 + 1) * SUBLANES]
            states[k0] = a0 * h + d0
            h = a2 * h + d2
            states[k1] = h
            p = a2 * p
        return h, p, states

    forward = list(range(SCAN_GROUP))
    backward = forward[::-1]

    def local_scan(g, carry):
        gb = n_groups - 1 - g
        out = []
        for lb, (hf, pf, hb, pb) in zip(blocks, carry):
            hf, pf, _ = scan_group(af_s[lb, group(g), :], df_s[lb, group(g), :], forward, hf, pf)
            hb, pb, _ = scan_group(ab_s[lb, group(gb), :], db_s[lb, group(gb), :], backward, hb, pb)
            out.append((hf, pf, hb, pb))
        return tuple(out)

    ends = lax.fori_loop(0, n_groups, local_scan, tuple((zeros, ones, zeros, ones) for _ in blocks))

    row = lax.broadcasted_iota(jnp.int32, (SUBLANES, LANES), 0)

    def pick(v, s):
        return jnp.sum(jnp.where(row == s, v, 0.0), axis=0, keepdims=True)

    entering = []
    for hf_last, pf_last, hb_first, pb_first in ends:
        cf = zeros
        for s in range(1, SUBLANES):
            nxt = pick(hf_last, s - 1) + pick(pf_last, s - 1) * pick(cf, s - 1)
            cf = jnp.where(row == s, nxt, cf)
        cbk = zeros
        for s in range(SUBLANES - 2, -1, -1):
            nxt = pick(hb_first, s + 1) + pick(pb_first, s + 1) * pick(cbk, s + 1)
            cbk = jnp.where(row == s, nxt, cbk)
        entering.append((cf, cbk))

    def full_scan(g, carry):
        gb = n_groups - 1 - g
        out = []
        for lb, (hf, hb) in zip(blocks, carry):
            hf, _, states = scan_group(af_s[lb, group(g), :], df_s[lb, group(g), :], forward, hf, ones)
            rf_s[lb, group(g), :] = jnp.concatenate([states[k] for k in forward], axis=0)
            hb, _, states = scan_group(ab_s[lb, group(gb), :], db_s[lb, group(gb), :], backward, hb, ones)
            rb_s[lb, group(gb), :] = jnp.concatenate([states[k] for k in forward], axis=0)
            out.append((hf, hb))
        return tuple(out)

    lax.fori_loop(0, n_groups, full_scan, tuple(entering))

    def emit(i, _):
        off = pl.multiple_of(i * ts, ts)
        for lb in blocks:
            y = yr_ref[pl.ds(off, ts), lanes(lb)].astype(F32)
            rec = rf_s[lb, interleaved(i), :] + rb_s[lb, interleaved(i), :]
            o_ref[pl.ds(off, ts), lanes(lb)] = (rec * jax.nn.gelu(y)).astype(o_ref.dtype)
        return 0

    lax.fori_loop(0, n_ts, emit, 0, unroll=math.gcd(n_ts, 8))


def _lru(xr, yr, conv_w, conv_b, wg, lru_lambda):
    B, S, _ = xr.shape
    width = LRU_LB * LANES
    chunk = S // SUBLANES
    ts = min(LRU_TS, chunk)
    assert S % SUBLANES == 0 and chunk % ts == 0 and chunk % SCAN_GROUP == 0 and LRU_WIDTH % width == 0
    col = lambda: pl.BlockSpec((None, S, width), lambda b, c: (b, 0, c))
    seq = lambda: pltpu.VMEM((LRU_LB, S, LANES), F32)
    return pl.pallas_call(
        functools.partial(_lru_kernel, ts=ts),
        grid=(B, LRU_WIDTH // width),
        in_specs=[
            col(),
            col(),
            pl.BlockSpec((CONV_WIDTH, width), lambda b, c: (0, c)),
            pl.BlockSpec((1, width), lambda b, c: (0, c)),
            pl.BlockSpec((LRU_LB, 2 * LANES, 4 * LANES), lambda b, c: (c, 0, 0)),
            pl.BlockSpec((2, width), lambda b, c: (0, c)),
        ],
        out_specs=col(),
        out_shape=jax.ShapeDtypeStruct((B, S, LRU_WIDTH), BF16),
        scratch_shapes=[
            pltpu.VMEM((LRU_LB, S + 2 * SUBLANES, LANES), F32),
            seq(), seq(), seq(), seq(), seq(), seq(),
        ],
        compiler_params=_cparams(2),
        name="lru",
    )(xr, yr, conv_w, conv_b.reshape(1, LRU_WIDTH), wg, lru_lambda)


def _pack_gate_weights(w_gates, b_gates):
    ncb = LRU_WIDTH // LANES
    per = LANES // LRU_BLOCK
    w = w_gates.reshape(4, ncb, per, LRU_BLOCK, LRU_BLOCK)
    eye = jnp.eye(per, dtype=w.dtype)
    dense = jnp.einsum("gcpde,pq->cpdgqe", w, eye)
    dense = (0.5 * dense.reshape(ncb, LANES, 4 * LANES)).astype(BF16)
    bias = 0.5 * b_gates.reshape(4, ncb, LANES).transpose(1, 0, 2).reshape(ncb, 1, 4 * LANES)
    bias_hi, bias_lo = _split_bf16(bias)
    pad = jnp.zeros((ncb, LANES - 2, 4 * LANES), BF16)
    return jnp.concatenate([dense, bias_hi, bias_lo, pad], axis=1)


OUT_ROWS = 256


def _out_kernel(x_ref, attn_ref, rec_ref, ga_ref, gl_ref, mod_ref, wab_ref, wlb_ref, wo_ref,
                wfi_ref, wfo_ref, ln_ref, y_ref):
    tm = x_ref.shape[0]
    groups = [slice(g * OUT_ROWS, (g + 1) * OUT_ROWS) for g in range(tm // OUT_ROWS)]

    def branches(rows):
        a = jnp.dot(attn_ref[rows, :], wab_ref[...], preferred_element_type=F32)
        r = jnp.dot(rec_ref[rows, :], wlb_ref[...], preferred_element_type=F32)
        merged = _sigmoid(ga_ref[rows, :].astype(F32)) * a + _sigmoid(gl_ref[rows, :].astype(F32)) * r
        return merged.astype(BF16)

    def mixer_norm(rows, merged):
        mix = jnp.dot(merged, wo_ref[...], preferred_element_type=F32)
        x1 = _layer_norm(ALPHA * x_ref[rows, :] + mod_ref[2:3, :] * mix) * ln_ref[0:1, :] + ln_ref[1:2, :]
        h = _layer_norm(x1) * (1.0 + mod_ref[4:5, :]) + mod_ref[3:4, :]
        return x1, h.astype(BF16)

    def ffn_hidden(h):
        gu = jnp.dot(h, wfi_ref[...], preferred_element_type=F32)
        gate = gu[:, :D_FF]
        return (gate * _sigmoid(gate) * gu[:, D_FF:]).astype(BF16)

    def ffn_norm(rows, x1, act):
        f = jnp.dot(act, wfo_ref[...], preferred_element_type=F32)
        y_ref[rows, :] = _layer_norm(ALPHA * x1 + mod_ref[5:6, :] * f) * ln_ref[2:3, :] + ln_ref[3:4, :]

    merged = [branches(rows) for rows in groups]
    normed = [mixer_norm(rows, m) for rows, m in zip(groups, merged)]
    acts = [ffn_hidden(h) for _, h in normed]
    for rows, (x1, _), act in zip(groups, normed, acts):
        ffn_norm(rows, x1, act)


def _out(x, attn, rec, ga, gl, mod, wab, wlb, wo, wfi, wfo, ln):
    B, S, _ = x.shape
    tm = min(2 * OUT_ROWS, S)
    assert S % tm == 0 and tm % OUT_ROWS == 0
    tok = lambda: pl.BlockSpec((None, tm, D_MODEL), lambda b, i: (b, i, 0))
    return pl.pallas_call(
        _out_kernel,
        grid=(B, S // tm),
        in_specs=[
            tok(), tok(), tok(), tok(), tok(),
            pl.BlockSpec((None, 6, D_MODEL), lambda b, i: (b, 0, 0)),
            _resident(wab.shape), _resident(wlb.shape), _resident(wo.shape),
            _resident(wfi.shape), _resident(wfo.shape),
            _resident(ln.shape),
        ],
        out_specs=tok(),
        out_shape=jax.ShapeDtypeStruct((B, S, D_MODEL), F32),
        compiler_params=_cparams(2),
        name="out",
    )(x, attn, rec, ga, gl, mod, wab, wlb, wo, wfi, wfo, ln)


def _rope_tables(seq):
    inv = 1.0 / (ROPE_THETA ** (jnp.arange(0, HEAD_DIM, 2, dtype=F32) / HEAD_DIM))
    ang = jnp.arange(seq, dtype=F32)[:, None] * inv[None, :]
    cos, sin = jnp.cos(ang), jnp.sin(ang)
    return (jnp.concatenate([cos, cos, cos, cos], axis=1),
            jnp.concatenate([-sin, -sin, sin, sin], axis=1))


def _permute_qk_columns(w):
    d = w.shape[0]
    return w.reshape(d, N_HEADS, 2, 2, HEAD_DIM // 2).transpose(0, 1, 3, 2, 4).reshape(d, N_HEADS * LANES)


def kernel(x_prompt, x_sample, c_prompt, c_sample, w_ada, b_ada, w_in, lambda_q1, lambda_k1, lambda_q2, lambda_k2, subln_g, conv_w, conv_b, w_lru_gates, b_lru_gates, lru_lambda, w_attn_branch, w_lru_branch, w_out, ln1_g, ln1_b, w_ffn_in, w_ffn_out, ln2_g, ln2_b):
    assert w_ada.shape[0] == DEPTH == 1
    lambda_init = 0.8 - 0.6 * math.exp(-0.3 * 0)

    w_in0 = w_in[0]
    w_in_b = jnp.concatenate(
        [_permute_qk_columns(w_in0[:, :D_MODEL]), _permute_qk_columns(w_in0[:, D_MODEL:2 * D_MODEL]),
         w_in0[:, 2 * D_MODEL:]], axis=1).astype(BF16)
    lam_vecs = jnp.concatenate([lambda_q1, lambda_k1, lambda_q2, lambda_k2], axis=0).astype(F32)
    wg = _pack_gate_weights(w_lru_gates[0], b_lru_gates[0])
    wab = w_attn_branch[0].astype(BF16)
    wlb = w_lru_branch[0].astype(BF16)
    wo = w_out[0].astype(BF16)
    wfi = w_ffn_in[0].astype(BF16)
    wfo = w_ffn_out[0].astype(BF16)
    ln = jnp.concatenate([ln1_g, ln1_b, ln2_g, ln2_b], axis=0).astype(F32)

    nb = c_prompt.shape[0]
    mod_all = _ada(jnp.concatenate([c_prompt, c_sample], axis=0), w_ada[0], b_ada[0])
    mod_all = mod_all.reshape(mod_all.shape[0], 6, D_MODEL)

    def trunk(x, mod):
        cos_t, sin_t = _rope_tables(x.shape[1])
        q, k, v, xr, yr, ga, gl = _inproj(x, mod, cos_t, sin_t, w_in_b)
        attn = _attention(q, k, v, lam_vecs, subln_g[0].astype(F32), lambda_init)
        rec = _lru(xr, yr, conv_w[0], conv_b[0], wg, lru_lambda[0])
        return _out(x, attn, rec, ga, gl, mod, wab, wlb, wo, wfi, wfo, ln)

    return (trunk(x_prompt, mod_all[:nb]), trunk(x_sample, mod_all[nb:]))
```

```python
import functools
import math

import jax
import jax.numpy as jnp
from jax import lax
from jax.experimental import pallas as pl
from jax.experimental.pallas import tpu as pltpu

D_MODEL = 1024
N_HEADS = 8
HEAD_DIM = 64
V_DIM = 2 * HEAD_DIM
ROPE_THETA = 10000.0
RMS_EPS = 1e-5
LRU_WIDTH = 1024
LRU_BLOCKS = 16
LRU_BLOCK = LRU_WIDTH // LRU_BLOCKS
LRU_C = 8.0
CONV_WIDTH = 4
CONV_LEFT = 2
D_FF = 2816
DEPTH = 1
ALPHA = (2.0 * DEPTH) ** 0.25
LN_EPS = 1e-5
LOG2E = 1.4426950408889634

LANES = 128
SUBLANES = 8
VMEM_LIMIT = 56 * 1024 * 1024

F32 = jnp.float32
BF16 = jnp.bfloat16


def _cparams(n_axes):
    return pltpu.CompilerParams(
        dimension_semantics=("arbitrary",) * n_axes,
        vmem_limit_bytes=VMEM_LIMIT,
    )


def _resident(shape):
    nd = len(shape)
    return pl.BlockSpec(shape, lambda *_: (0,) * nd, pipeline_mode=pl.Buffered(1))


def _layer_norm(x):
    mu = jnp.mean(x, axis=-1, keepdims=True)
    xc = x - mu
    var = jnp.mean(xc * xc, axis=-1, keepdims=True)
    return xc * lax.rsqrt(var + LN_EPS)


def _sigmoid(x):
    return 0.5 * jnp.tanh(0.5 * x) + 0.5


def _split_bf16(x):
    hi = x.astype(BF16)
    lo = (x - hi.astype(F32)).astype(BF16)
    return hi, lo


def _ada_kernel(c_ref, w_ref, b_ref, o_ref):
    c = c_ref[...]
    s_hi, s_lo = _split_bf16(c * _sigmoid(c))
    w_hi, w_lo = _split_bf16(w_ref[...])
    acc = jnp.dot(s_hi, w_hi, preferred_element_type=F32)
    acc += jnp.dot(s_lo, w_hi, preferred_element_type=F32)
    acc += jnp.dot(s_hi, w_lo, preferred_element_type=F32)
    o_ref[...] = acc + b_ref[...]


def _ada(c, w_ada, b_ada):
    rows = c.shape[0]
    n_out = w_ada.shape[1]
    tn = 1024
    return pl.pallas_call(
        _ada_kernel,
        grid=(n_out // tn,),
        in_specs=[
            pl.BlockSpec((rows, D_MODEL), lambda j: (0, 0)),
            pl.BlockSpec((D_MODEL, tn), lambda j: (0, j)),
            pl.BlockSpec((1, tn), lambda j: (0, j)),
        ],
        out_specs=pl.BlockSpec((rows, tn), lambda j: (0, j)),
        out_shape=jax.ShapeDtypeStruct((rows, n_out), F32),
        compiler_params=_cparams(1),
        name="ada",
    )(c, w_ada, b_ada.reshape(1, n_out))


N_PROJ = 7
IN_ROWS = 256


def _inproj_kernel(x_ref, mod_ref, cos_ref, sin_ref, w_ref, *out_refs):
    tm = x_ref.shape[0]
    groups = [slice(r * IN_ROWS, (r + 1) * IN_ROWS) for r in range(tm // IN_ROWS)]
    hbs = []
    for rows in groups:
        h = _layer_norm(x_ref[rows, :]) * (1.0 + mod_ref[1:2, :]) + mod_ref[0:1, :]
        hbs.append(h.astype(BF16))
    q_scale = HEAD_DIM ** -0.5 * LOG2E
    for g, o_ref in enumerate(out_refs):
        for rows, hb in zip(groups, hbs):
            p = jnp.dot(hb, w_ref[:, g * D_MODEL:(g + 1) * D_MODEL], preferred_element_type=F32)
            if g < 2:
                cos = cos_ref[rows, :]
                sin = sin_ref[rows, :]
                for hh in range(N_HEADS):
                    t = p[:, hh * LANES:(hh + 1) * LANES]
                    r = t * cos + pltpu.roll(t, LANES // 2, axis=1) * sin
                    if g == 0:
                        r = r * q_scale
                    o_ref[rows, hh * LANES:(hh + 1) * LANES] = r.astype(o_ref.dtype)
            else:
                o_ref[rows, :] = p.astype(o_ref.dtype)


def _inproj(x, mod, cos_t, sin_t, w_in_b):
    B, S, _ = x.shape
    tm = min(2 * IN_ROWS, S)
    assert S % tm == 0 and tm % IN_ROWS == 0
    tok = lambda: pl.BlockSpec((None, tm, D_MODEL), lambda b, i: (b, i, 0))
    return pl.pallas_call(
        _inproj_kernel,
        grid=(B, S // tm),
        in_specs=[
            tok(),
            pl.BlockSpec((None, 6, D_MODEL), lambda b, i: (b, 0, 0)),
            pl.BlockSpec((tm, LANES), lambda b, i: (i, 0)),
            pl.BlockSpec((tm, LANES), lambda b, i: (i, 0)),
            _resident(w_in_b.shape),
        ],
        out_specs=[tok() for _ in range(N_PROJ)],
        out_shape=[jax.ShapeDtypeStruct((B, S, D_MODEL), BF16) for _ in range(N_PROJ)],
        compiler_params=_cparams(2),
        name="inproj",
    )(x, mod, cos_t, sin_t, w_in_b)


ONES_ROWS = 16


def _attn_kernel(lam_ref, g_ref, q_ref, k_ref, v_ref, qn_ref, kn_ref, o_ref, vt_ref, s_ref, acc_ref, m_ref,
                 *, tq, kc, lambda_init):
    S = k_ref.shape[0]
    nq = S // tq
    nc = S // kc

    vt_ref[0:V_DIM, :] = v_ref[...].astype(F32).T.astype(BF16)
    vt_ref[V_DIM:V_DIM + ONES_ROWS, :] = jnp.ones((ONES_ROWS, S), BF16)

    lv = lam_ref[...]
    lam = (jnp.exp(jnp.sum(lv[0:1] * lv[1:2], axis=1, keepdims=True))
           - jnp.exp(jnp.sum(lv[2:3] * lv[3:4], axis=1, keepdims=True)) + lambda_init)

    lane = lax.broadcasted_iota(jnp.int32, (tq, LANES), 1)
    is_map0 = (lane % (2 * 32)) < 32

    def mask_maps(q):
        zero = jnp.zeros_like(q)
        return jnp.concatenate([jnp.where(is_map0, q, zero), jnp.where(is_map0, zero, q)], axis=0)

    def masked_queries(i):
        return mask_maps(q_ref[pl.ds(pl.multiple_of(i * tq, tq), tq), :])

    def score_chunk(keys_ref, qm, slot, c, m):
        s = lax.dot_general(keys_ref[c * kc:(c + 1) * kc, :], qm, (((1,), (1,)), ((), ())),
                            preferred_element_type=F32)
        s_ref[slot, c * kc:(c + 1) * kc, :] = s
        m_c = jnp.max(s, axis=0, keepdims=True)
        return m_c if m is None else jnp.maximum(m, m_c)

    def value_chunk(slot, c, m):
        e = jnp.exp2(s_ref[slot, c * kc:(c + 1) * kc, :] - m).astype(BF16)
        part = jnp.dot(vt_ref[:, c * kc:(c + 1) * kc], e, preferred_element_type=F32)
        if c == 0:
            acc_ref[slot] = part
        else:
            acc_ref[slot] += part

    def finish(i, slot):
        acc = acc_ref[slot]
        ot = acc[0:V_DIM] * (1.0 / acc[V_DIM:V_DIM + 1])
        o = ot[:, :tq] - lam * ot[:, tq:]
        o = o * lax.rsqrt(jnp.mean(o * o, axis=0, keepdims=True) + RMS_EPS)
        o = o * g_ref[...] * (1.0 - lambda_init)
        o_ref[pl.ds(pl.multiple_of(i * tq, tq), tq), :] = o.T.astype(o_ref.dtype)

    def scores_only(i, slot):
        qm = masked_queries(i)
        m = None
        for c in range(nc):
            m = score_chunk(k_ref, qm, slot, c, m)
        return m

    NEXT_TILE, NEXT_STEP = "tile", "step"

    def value_phase(i, slot, m, next_scores, finish_previous):
        if next_scores == NEXT_TILE:
            keys_ref, qm_next = k_ref, masked_queries(i + 1)
        elif next_scores == NEXT_STEP:
            keys_ref, qm_next = kn_ref, mask_maps(qn_ref[...])
        m_next = None
        for c in range(nc):
            if next_scores is not None:
                m_next = score_chunk(keys_ref, qm_next, 1 - slot, c, m_next)
            value_chunk(slot, c, m)
            if c == 0 and finish_previous:
                finish(i - 1, 1 - slot)
        return m_next

    if nq == 1:
        value_phase(0, 0, scores_only(0, 0), None, False)
        finish(0, 0)
    else:
        @pl.when((pl.program_id(0) == 0) & (pl.program_id(1) == 0))
        def _():
            m_ref[...] = scores_only(0, 0)

        m_odd = value_phase(0, 0, m_ref[...], NEXT_TILE, False)

        def pair(j, m):
            m_e = value_phase(2 * j + 1, 1, m, NEXT_TILE, True)
            return value_phase(2 * j + 2, 0, m_e, NEXT_TILE, True)

        m_odd = lax.fori_loop(0, nq // 2 - 1, pair, m_odd, unroll=(nq // 2 - 1) <= 3)
        m_ref[...] = value_phase(nq - 1, 1, m_odd, NEXT_STEP, True)
        finish(nq - 1, 1)


def _attention(q, k, v, lam_vecs, subln_g, lambda_init):
    B, S, _ = q.shape
    tq = min(256, S)
    kc = min(512, S)
    nq = S // tq
    assert S % tq == 0 and S % kc == 0 and (nq == 1 or nq % 2 == 0)
    col = lambda: pl.BlockSpec((None, S, LANES), lambda b, h: (b, 0, h))

    def next_step(b, h):
        t = jnp.minimum(b * N_HEADS + h + 1, B * N_HEADS - 1)
        return t // N_HEADS, 0, t % N_HEADS

    return pl.pallas_call(
        functools.partial(_attn_kernel, tq=tq, kc=kc, lambda_init=lambda_init),
        grid=(B, N_HEADS),
        in_specs=[
            pl.BlockSpec((4, HEAD_DIM), lambda b, h: (0, 0)),
            pl.BlockSpec((V_DIM, 1), lambda b, h: (0, 0)),
            col(), col(), col(),
            pl.BlockSpec((None, tq, LANES), next_step),
            pl.BlockSpec((None, S, LANES), next_step),
        ],
        out_specs=col(),
        out_shape=jax.ShapeDtypeStruct((B, S, D_MODEL), BF16),
        scratch_shapes=[
            pltpu.VMEM((V_DIM + ONES_ROWS, S), BF16),
            pltpu.VMEM((2, S, 2 * tq), F32),
            pltpu.VMEM((2, V_DIM + ONES_ROWS, 2 * tq), F32),
            pltpu.VMEM((1, 2 * tq), F32),
        ],
        compiler_params=_cparams(2),
        name="attn",
    )(lam_vecs, subln_g.reshape(V_DIM, 1), q, k, v, q, k)


LRU_TS = 256
LRU_LB = 2
SCAN_GROUP = 16


def _lru_kernel(xr_ref, yr_ref, cw_ref, cb_ref, wg_ref, lam_ref, o_ref,
                x_s, af_s, df_s, ab_s, db_s, rf_s, rb_s, *, ts):
    S = xr_ref.shape[0]
    chunk = S // SUBLANES
    n_ts = S // ts
    blocks = range(LRU_LB)

    def lanes(lb):
        return slice(lb * LANES, (lb + 1) * LANES)

    def interleaved(i):
        first = i * ts
        return pl.ds((first % chunk) * SUBLANES + first // chunk, ts, stride=SUBLANES)

    for lb in blocks:
        x_s[lb, 0:SUBLANES, :] = jnp.zeros((SUBLANES, LANES), F32)
        x_s[lb, S + SUBLANES:S + 2 * SUBLANES, :] = jnp.zeros((SUBLANES, LANES), F32)

    def stage(i, _):
        off = pl.multiple_of(i * ts, ts)
        for lb in blocks:
            x_s[lb, pl.ds(off + SUBLANES, ts), :] = xr_ref[pl.ds(off, ts), lanes(lb)].astype(F32)
        return 0

    lax.fori_loop(0, n_ts, stage, 0, unroll=math.gcd(n_ts, 4))

    cw = cw_ref[...]
    cb = cb_ref[...]
    z = -lam_ref[...]
    softplus = jnp.maximum(z, 0.0) + jnp.log1p(jnp.exp(-jnp.abs(z)))
    half_neg_c_sp = (-0.5 * LRU_C) * softplus
    lane = lax.broadcasted_iota(jnp.int32, (ts, LANES), 1)
    bias_lanes = jnp.where(lane < 2, 1.0, 0.0).astype(BF16)
    tiny = float(jnp.finfo(F32).tiny)

    def gates(i, _):
        off = pl.multiple_of(i * ts, ts)
        for lb in blocks:
            xc = cb[:, lanes(lb)]
            for tap in range(CONV_WIDTH):
                xc = xc + cw[tap:tap + 1, lanes(lb)] * x_s[lb, pl.ds(off + SUBLANES + tap - CONV_LEFT, ts), :]
            lhs = jnp.concatenate([xc.astype(BF16), bias_lanes], axis=1)
            g = jnp.dot(lhs, wg_ref[lb], preferred_element_type=F32)
            sx = -math.sqrt(0.5) * xc
            for d, (a_s, d_s) in enumerate(((af_s, df_s), (ab_s, db_s))):
                k = half_neg_c_sp[d:d + 1, lanes(lb)]
                log_a = k * jnp.tanh(g[:, (2 * d) * LANES:(2 * d + 1) * LANES]) + k
                a_s[lb, interleaved(i), :] = jnp.exp(log_a)
                gated_x = sx * jnp.tanh(g[:, (2 * d + 1) * LANES:(2 * d + 2) * LANES]) + sx
                th = jnp.tanh(log_a)
                root = th * lax.rsqrt(jnp.maximum(th * (th - 1.0), tiny))
                d_s[lb, interleaved(i), :] = root * gated_x
        return 0

    lax.fori_loop(0, n_ts, gates, 0, unroll=math.gcd(n_ts, 8))

    zeros = jnp.zeros((SUBLANES, LANES), F32)
    ones = jnp.ones((SUBLANES, LANES), F32)

    n_groups = chunk // SCAN_GROUP
    group_rows = SCAN_GROUP * SUBLANES

    def group(g):
        return pl.ds(pl.multiple_of(g * group_rows, group_rows), group_rows)

    def scan_group(a, d, order, h, p):
        states = {}
        for k0, k1 in zip(order[0::2], order[1::2]):
            a0 = a[k0 * SUBLANES:(k0 + 1) * SUBLANES]
            d0 = d[k0 * SUBLANES:(k0 + 1) * SUBLANES]
            a1 = a[k1 * SUBLANES:(k1 + 1) * SUBLANES]
            a2 = a1 * a0
            d2 = a1 * d0 + d[k1 * SUBLANES:(k1 + 1) * SUBLANES]
            states[k0] = a0 * h + d0
            h = a2 * h + d2
            states[k1] = h
            p = a2 * p
        return h, p, states

    forward = list(range(SCAN_GROUP))
    backward = forward[::-1]

    def local_scan(g, carry):
        gb = n_groups - 1 - g
        out = []
        for lb, (hf, pf, hb, pb) in zip(blocks, carry):
            hf, pf, _ = scan_group(af_s[lb, group(g), :], df_s[lb, group(g), :], forward, hf, pf)
            hb, pb, _ = scan_group(ab_s[lb, group(gb), :], db_s[lb, group(gb), :], backward, hb, pb)
            out.append((hf, pf, hb, pb))
        return tuple(out)

    ends = lax.fori_loop(0, n_groups, local_scan, tuple((zeros, ones, zeros, ones) for _ in blocks))

    row = lax.broadcasted_iota(jnp.int32, (SUBLANES, LANES), 0)

    def pick(v, s):
        return jnp.sum(jnp.where(row == s, v, 0.0), axis=0, keepdims=True)

    entering = []
    for hf_last, pf_last, hb_first, pb_first in ends:
        cf = zeros
        for s in range(1, SUBLANES):
            nxt = pick(hf_last, s - 1) + pick(pf_last, s - 1) * pick(cf, s - 1)
            cf = jnp.where(row == s, nxt, cf)
        cbk = zeros
        for s in range(SUBLANES - 2, -1, -1):
            nxt = pick(hb_first, s + 1) + pick(pb_first, s + 1) * pick(cbk, s + 1)
            cbk = jnp.where(row == s, nxt, cbk)
        entering.append((cf, cbk))

    def full_scan(g, carry):
        gb = n_groups - 1 - g
        out = []
        for lb, (hf, hb) in zip(blocks, carry):
            hf, _, states = scan_group(af_s[lb, group(g), :], df_s[lb, group(g), :], forward, hf, ones)
            rf_s[lb, group(g), :] = jnp.concatenate([states[k] for k in forward], axis=0)
            hb, _, states = scan_group(ab_s[lb, group(gb), :], db_s[lb, group(gb), :], backward, hb, ones)
            rb_s[lb, group(gb), :] = jnp.concatenate([states[k] for k in forward], axis=0)
            out.append((hf, hb))
        return tuple(out)

    lax.fori_loop(0, n_groups, full_scan, tuple(entering))

    def emit(i, _):
        off = pl.multiple_of(i * ts, ts)
        for lb in blocks:
            y = yr_ref[pl.ds(off, ts), lanes(lb)].astype(F32)
            rec = rf_s[lb, interleaved(i), :] + rb_s[lb, interleaved(i), :]
            o_ref[pl.ds(off, ts), lanes(lb)] = (rec * jax.nn.gelu(y)).astype(o_ref.dtype)
        return 0

    lax.fori_loop(0, n_ts, emit, 0, unroll=math.gcd(n_ts, 8))


def _lru(xr, yr, conv_w, conv_b, wg, lru_lambda):
    B, S, _ = xr.shape
    width = LRU_LB * LANES
    chunk = S // SUBLANES
    ts = min(LRU_TS, chunk)
    assert S % SUBLANES == 0 and chunk % ts == 0 and chunk % SCAN_GROUP == 0 and LRU_WIDTH % width == 0
    col = lambda: pl.BlockSpec((None, S, width), lambda b, c: (b, 0, c))
    seq = lambda: pltpu.VMEM((LRU_LB, S, LANES), F32)
    return pl.pallas_call(
        functools.partial(_lru_kernel, ts=ts),
        grid=(B, LRU_WIDTH // width),
        in_specs=[
            col(),
            col(),
            pl.BlockSpec((CONV_WIDTH, width), lambda b, c: (0, c)),
            pl.BlockSpec((1, width), lambda b, c: (0, c)),
            pl.BlockSpec((LRU_LB, 2 * LANES, 4 * LANES), lambda b, c: (c, 0, 0)),
            pl.BlockSpec((2, width), lambda b, c: (0, c)),
        ],
        out_specs=col(),
        out_shape=jax.ShapeDtypeStruct((B, S, LRU_WIDTH), BF16),
        scratch_shapes=[
            pltpu.VMEM((LRU_LB, S + 2 * SUBLANES, LANES), F32),
            seq(), seq(), seq(), seq(), seq(), seq(),
        ],
        compiler_params=_cparams(2),
        name="lru",
    )(xr, yr, conv_w, conv_b.reshape(1, LRU_WIDTH), wg, lru_lambda)


def _pack_gate_weights(w_gates, b_gates):
    ncb = LRU_WIDTH // LANES
    per = LANES // LRU_BLOCK
    w = w_gates.reshape(4, ncb, per, LRU_BLOCK, LRU_BLOCK)
    eye = jnp.eye(per, dtype=w.dtype)
    dense = jnp.einsum("gcpde,pq->cpdgqe", w, eye)
    dense = (0.5 * dense.reshape(ncb, LANES, 4 * LANES)).astype(BF16)
    bias = 0.5 * b_gates.reshape(4, ncb, LANES).transpose(1, 0, 2).reshape(ncb, 1, 4 * LANES)
    bias_hi, bias_lo = _split_bf16(bias)
    pad = jnp.zeros((ncb, LANES - 2, 4 * LANES), BF16)
    return jnp.concatenate([dense, bias_hi, bias_lo, pad], axis=1)


OUT_ROWS = 256


def _out_kernel(x_ref, attn_ref, rec_ref, ga_ref, gl_ref, mod_ref, wab_ref, wlb_ref, wo_ref,
                wfi_ref, wfo_ref, ln_ref, y_ref):
    tm = x_ref.shape[0]
    groups = [slice(g * OUT_ROWS, (g + 1) * OUT_ROWS) for g in range(tm // OUT_ROWS)]

    def branches(rows):
        a = jnp.dot(attn_ref[rows, :], wab_ref[...], preferred_element_type=F32)
        r = jnp.dot(rec_ref[rows, :], wlb_ref[...], preferred_element_type=F32)
        merged = _sigmoid(ga_ref[rows, :].astype(F32)) * a + _sigmoid(gl_ref[rows, :].astype(F32)) * r
        return merged.astype(BF16)

    def mixer_norm(rows, merged):
        mix = jnp.dot(merged, wo_ref[...], preferred_element_type=F32)
        x1 = _layer_norm(ALPHA * x_ref[rows, :] + mod_ref[2:3, :] * mix) * ln_ref[0:1, :] + ln_ref[1:2, :]
        h = _layer_norm(x1) * (1.0 + mod_ref[4:5, :]) + mod_ref[3:4, :]
        return x1, h.astype(BF16)

    def ffn_hidden(h):
        gu = jnp.dot(h, wfi_ref[...], preferred_element_type=F32)
        gate = gu[:, :D_FF]
        return (gate * _sigmoid(gate) * gu[:, D_FF:]).astype(BF16)

    def ffn_norm(rows, x1, act):
        f = jnp.dot(act, wfo_ref[...], preferred_element_type=F32)
        y_ref[rows, :] = _layer_norm(ALPHA * x1 + mod_ref[5:6, :] * f) * ln_ref[2:3, :] + ln_ref[3:4, :]

    merged = [branches(rows) for rows in groups]
    normed = [mixer_norm(rows, m) for rows, m in zip(groups, merged)]
    acts = [ffn_hidden(h) for _, h in normed]
    for rows, (x1, _), act in zip(groups, normed, acts):
        ffn_norm(rows, x1, act)


def _out(x, attn, rec, ga, gl, mod, wab, wlb, wo, wfi, wfo, ln):
    B, S, _ = x.shape
    tm = min(2 * OUT_ROWS, S)
    assert S % tm == 0 and tm % OUT_ROWS == 0
    tok = lambda: pl.BlockSpec((None, tm, D_MODEL), lambda b, i: (b, i, 0))
    return pl.pallas_call(
        _out_kernel,
        grid=(B, S // tm),
        in_specs=[
            tok(), tok(), tok(), tok(), tok(),
            pl.BlockSpec((None, 6, D_MODEL), lambda b, i: (b, 0, 0)),
            _resident(wab.shape), _resident(wlb.shape), _resident(wo.shape),
            _resident(wfi.shape), _resident(wfo.shape),
            _resident(ln.shape),
        ],
        out_specs=tok(),
        out_shape=jax.ShapeDtypeStruct((B, S, D_MODEL), F32),
        compiler_params=_cparams(2),
        name="out",
    )(x, attn, rec, ga, gl, mod, wab, wlb, wo, wfi, wfo, ln)


def _rope_tables(seq):
    inv = 1.0 / (ROPE_THETA ** (jnp.arange(0, HEAD_DIM, 2, dtype=F32) / HEAD_DIM))
    ang = jnp.arange(seq, dtype=F32)[:, None] * inv[None, :]
    cos, sin = jnp.cos(ang), jnp.sin(ang)
    return (jnp.concatenate([cos, cos, cos, cos], axis=1),
            jnp.concatenate([-sin, -sin, sin, sin], axis=1))


def _permute_qk_columns(w):
    d = w.shape[0]
    return w.reshape(d, N_HEADS, 2, 2, HEAD_DIM // 2).transpose(0, 1, 3, 2, 4).reshape(d, N_HEADS * LANES)


def kernel(x_prompt, x_sample, c_prompt, c_sample, w_ada, b_ada, w_in, lambda_q1, lambda_k1, lambda_q2, lambda_k2, subln_g, conv_w, conv_b, w_lru_gates, b_lru_gates, lru_lambda, w_attn_branch, w_lru_branch, w_out, ln1_g, ln1_b, w_ffn_in, w_ffn_out, ln2_g, ln2_b):
    assert w_ada.shape[0] == DEPTH == 1
    lambda_init = 0.8 - 0.6 * math.exp(-0.3 * 0)

    w_in0 = w_in[0]
    w_in_b = jnp.concatenate(
        [_permute_qk_columns(w_in0[:, :D_MODEL]), _permute_qk_columns(w_in0[:, D_MODEL:2 * D_MODEL]),
         w_in0[:, 2 * D_MODEL:]], axis=1).astype(BF16)
    lam_vecs = jnp.concatenate([lambda_q1, lambda_k1, lambda_q2, lambda_k2], axis=0).astype(F32)
    wg = _pack_gate_weights(w_lru_gates[0], b_lru_gates[0])
    wab = w_attn_branch[0].astype(BF16)
    wlb = w_lru_branch[0].astype(BF16)
    wo = w_out[0].astype(BF16)
    wfi = w_ffn_in[0].astype(BF16)
    wfo = w_ffn_out[0].astype(BF16)
    ln = jnp.concatenate([ln1_g, ln1_b, ln2_g, ln2_b], axis=0).astype(F32)

    nb = c_prompt.shape[0]
    mod_all = _ada(jnp.concatenate([c_prompt, c_sample], axis=0), w_ada[0], b_ada[0])
    mod_all = mod_all.reshape(mod_all.shape[0], 6, D_MODEL)

    def trunk(x, mod):
        cos_t, sin_t = _rope_tables(x.shape[1])
        q, k, v, xr, yr, ga, gl = _inproj(x, mod, cos_t, sin_t, w_in_b)
        attn = _attention(q, k, v, lam_vecs, subln_g[0].astype(F32), lambda_init)
        rec = _lru(xr, yr, conv_w[0], conv_b[0], wg, lru_lambda[0])
        return _out(x, attn, rec, ga, gl, mod, wab, wlb, wo, wfi, wfo, ln)

    return (trunk(x_prompt, mod_all[:nb]), trunk(x_sample, mod_all[nb:]))
```
